```python
import math
import jax, jax.numpy as jnp
from jax import lax
import numpy as np

D_MODEL = 1024
BATCH = 16
SEQ = 2048
DEPTH = 2
DEC_BATCH = 32
DEC_SEQ = 16
PAST_LEN = 1024

CHUNK = 64
Q_BLOCK = 128
EPS = 1e-6
N_MOD = 6
POOL_GROUPS = 4
POOL_GDIM = 64
POOL_WIDTH = POOL_GROUPS * POOL_GDIM
POOL_WINDOWS = (2, 4, 8, 16)
POOL_HIST = max(POOL_WINDOWS) - 1
GMLP_GROUPS = 4
GMLP_GDIM = 64
GMLP_WIDTH = GMLP_GROUPS * GMLP_GDIM
GMLP_CHUNK = 128
ATT_HEADS = 4
ATT_QK_DIM = 64
ATT_V_DIM = 128
ATT_QK_WIDTH = ATT_HEADS * 2 * ATT_QK_DIM
ATT_WIDTH = ATT_HEADS * ATT_V_DIM
MIX_WIDTH = POOL_WIDTH + GMLP_WIDTH + ATT_WIDTH
IN_WIDTH = POOL_WIDTH + 2 * GMLP_WIDTH + 2 * ATT_QK_WIDTH + ATT_WIDTH
SPLIT_POINTS = (POOL_WIDTH, POOL_WIDTH + GMLP_WIDTH, POOL_WIDTH + 2 * GMLP_WIDTH,
                POOL_WIDTH + 2 * GMLP_WIDTH + ATT_QK_WIDTH,
                POOL_WIDTH + 2 * GMLP_WIDTH + 2 * ATT_QK_WIDTH)
REL_BUCKETS = 32
REL_MAX_DIST = 128
PEER_HEADS = 8
PEER_NKEYS = 128
PEER_EXPERTS = PEER_NKEYS * PEER_NKEYS
PEER_HALF = 128
PEER_QDIM = 2 * PEER_HALF
PEER_TOPK = 16
PEER_BLOCK = 128

kernel_name = 'hymba_pool_gmlp_diffattn_peer_stream'


def rmsnorm(x, g):
    xf = x.astype(jnp.float32)
    y = xf * lax.rsqrt(jnp.mean(xf * xf, axis=-1, keepdims=True) + EPS)
    return (y * g.astype(jnp.float32)).astype(x.dtype)


def ada_mod(c, w_ada, b_ada):
    m = jax.nn.silu(c) @ w_ada + b_ada
    return jnp.split(m[:, None, :], N_MOD, axis=-1)


def t5_bucket(rel):
    nb = REL_BUCKETS // 2
    max_exact = nb // 2
    side = jnp.where(rel > 0, nb, 0)
    n = jnp.abs(rel)
    nf = jnp.maximum(n, 1).astype(jnp.float32)
    large = max_exact + (jnp.log(nf / max_exact) / math.log(REL_MAX_DIST / max_exact)
                         * (nb - max_exact)).astype(jnp.int32)
    large = jnp.minimum(large, nb - 1)
    return side + jnp.where(n < max_exact, n, large)


def pool_mix(za, hist, pos0, w_pool, s_pool):
    B, T, _ = za.shape
    zf = jnp.concatenate([hist, za], axis=1).astype(jnp.float32)
    cs = jnp.concatenate([jnp.zeros_like(zf[:, :1]), jnp.cumsum(zf, axis=1)], axis=1)
    pos = pos0 + jnp.arange(T)
    means = []
    for g, w in enumerate(POOL_WINDOWS):
        sl = slice(g * POOL_GDIM, (g + 1) * POOL_GDIM)
        s = (cs[:, POOL_HIST + 1:POOL_HIST + 1 + T, sl]
             - cs[:, POOL_HIST + 1 - w:POOL_HIST + 1 - w + T, sl])
        cnt = jnp.minimum(w, pos + 1).astype(jnp.float32)
        means.append(s / cnt[None, :, None])
    d = (jnp.concatenate(means, axis=-1) - za.astype(jnp.float32)).reshape(B, T, POOL_GROUPS, POOL_GDIM)
    y = jnp.einsum('btgc,gcd->btgd', d, w_pool.astype(jnp.float32)).reshape(B, T, POOL_WIDTH)
    return (y * s_pool.astype(jnp.float32)).astype(za.dtype)


def gmlp_mix(zu, zv, g_gv, w_s, b_s):
    B, T, _ = zv.shape
    L = min(T, GMLP_CHUNK)
    vn = rmsnorm(zv.reshape(B, T, GMLP_GROUPS, GMLP_GDIM), g_gv.reshape(GMLP_GROUPS, GMLP_GDIM))
    vr = vn.reshape(B, T // L, L, GMLP_GROUPS, GMLP_GDIM)
    ws = w_s[:, :L, :L] * jnp.tril(jnp.ones((L, L), w_s.dtype))
    sp = jnp.einsum('hij,bnjhc->bnihc', ws, vr) + b_s[:, :L].T[None, None, :, :, None]
    y = zu * sp.reshape(B, T, GMLP_WIDTH)
    return y, vn.reshape(B, T, GMLP_WIDTH)


def diff_attn_block(q, k, v, q_pos, k_pos, visible, rel_bias, lam):
    logits = jnp.einsum('bqhjd,bkhjd->bhjqk', q.astype(jnp.float32), k.astype(jnp.float32)) * (ATT_QK_DIM ** -0.5)
    bias = rel_bias.astype(jnp.float32)[t5_bucket(k_pos[None, :] - q_pos[:, None])]
    logits = logits + jnp.transpose(bias, (2, 0, 1))[None, :, None]
    if visible is not None:
        logits = jnp.where(visible, logits, -1e30)
    p = jax.nn.softmax(logits, axis=-1)
    a = p[:, :, 0] - lam * p[:, :, 1]
    return jnp.einsum('bhqk,bkhe->bqhe', a, v.astype(jnp.float32))


def diff_attn_prompt(q, k, v, rel_bias, lam):
    T = q.shape[1]
    pos = jnp.arange(T)
    outs = []
    for qb in range(T // Q_BLOCK):
        q0, q1 = qb * Q_BLOCK, (qb + 1) * Q_BLOCK
        qp, kp = pos[q0:q1], pos[:q1]
        vis = (kp[None, :] // CHUNK) <= (qp[:, None] // CHUNK)
        outs.append(diff_attn_block(q[:, q0:q1], k[:, :q1], v[:, :q1], qp, kp, vis, rel_bias, lam))
    return jnp.concatenate(outs, axis=1)


def peer_ffn(h, w_pq, sub_keys, u_tab, v_tab):
    B, T, D = h.shape
    n = B * T
    hf = h.reshape(n, D)
    q = (hf @ w_pq).reshape(n, PEER_HEADS, 2, PEER_HALF)
    s = jnp.einsum('nhjd,hjkd->nhjk', q.astype(jnp.float32), sub_keys.astype(jnp.float32))
    sv, si = lax.top_k(s, PEER_TOPK)
    cand = (sv[:, :, 0, :, None] + sv[:, :, 1, None, :]).reshape(n, PEER_HEADS, PEER_TOPK * PEER_TOPK)
    cidx = (si[:, :, 0, :, None] * PEER_NKEYS + si[:, :, 1, None, :]).reshape(n, PEER_HEADS, PEER_TOPK * PEER_TOPK)
    tv, ti = lax.top_k(cand, PEER_TOPK)
    eidx = jnp.take_along_axis(cidx, ti, axis=-1).reshape(n, PEER_HEADS * PEER_TOPK)
    gate = jax.nn.softmax(tv, axis=-1).reshape(n, PEER_HEADS * PEER_TOPK).astype(h.dtype)
    nb = -(-n // PEER_BLOCK)
    pad = nb * PEER_BLOCK - n
    hb = jnp.pad(hf, ((0, pad), (0, 0))).reshape(nb, PEER_BLOCK, D)
    eb = jnp.pad(eidx, ((0, pad), (0, 0))).reshape(nb, PEER_BLOCK, PEER_HEADS * PEER_TOPK)
    gb = jnp.pad(gate, ((0, pad), (0, 0))).reshape(nb, PEER_BLOCK, PEER_HEADS * PEER_TOPK)

    def block(args):
        hx, ei, gt = args
        a = jnp.einsum('tkd,td->tk', u_tab[ei], hx)
        w = gt * jax.nn.gelu(a, approximate=False)
        return jnp.einsum('tk,tkd->td', w, v_tab[ei])

    out = lax.map(block, (hb, eb, gb)).reshape(nb * PEER_BLOCK, D)[:n]
    return out.reshape(B, T, D)


def run_trunk(x, c, state_pool, cache_k, cache_v, rel_bias, w_ada, b_ada, g_norm, w_in, w_out,
              w_pool, s_pool, g_gv, w_s, b_s, g_qk, lam_vecs, g_sub, w_pq, sub_keys, u_tab, v_tab):
    B, T, _ = x.shape
    sample = cache_k is not None
    pos0 = cache_k.shape[2] if sample else 0
    k_out, v_out, pool_out, gv_out = [], [], [], []
    for l in range(DEPTH):
        sh1, sc1, gt1, sh2, sc2, gt2 = ada_mod(c, w_ada[l], b_ada[l])
        h = rmsnorm(x, g_norm[l, 0]) * (1 + sc1) + sh1
        z = h @ w_in[l]
        za, zu, zv, zq, zk, zvv = jnp.split(z, SPLIT_POINTS, axis=-1)
        hist = state_pool[l] if sample else jnp.zeros((B, POOL_HIST, POOL_WIDTH), za.dtype)
        ya = pool_mix(za, hist, pos0, w_pool[l], s_pool[l])
        pool_out.append(jnp.concatenate([hist, za], axis=1)[:, -POOL_HIST:])
        yb, vn = gmlp_mix(zu, zv, g_gv[l], w_s[l], b_s[l])
        gv_out.append(vn)
        q = rmsnorm(zq.reshape(B, T, ATT_HEADS, 2, ATT_QK_DIM), g_qk[l, 0])
        k = rmsnorm(zk.reshape(B, T, ATT_HEADS, 2, ATT_QK_DIM), g_qk[l, 1])
        v = zvv.reshape(B, T, ATT_HEADS, ATT_V_DIM)
        lam_init = 0.8 - 0.6 * math.exp(-0.3 * l)
        lv = lam_vecs[l].astype(jnp.float32)
        lam = jnp.exp(jnp.sum(lv[0] * lv[1])) - jnp.exp(jnp.sum(lv[2] * lv[3])) + lam_init
        if sample:
            P = cache_k.shape[2]
            k_all = jnp.concatenate([cache_k[l].reshape(B, P, ATT_HEADS, 2, ATT_QK_DIM), k], axis=1)
            v_all = jnp.concatenate([cache_v[l], v], axis=1)
            o = diff_attn_block(q, k_all, v_all, P + jnp.arange(T), jnp.arange(P + T), None, rel_bias, lam)
        else:
            o = diff_attn_prompt(q, k, v, rel_bias, lam)
        o = (rmsnorm(o, g_sub[l].reshape(ATT_HEADS, ATT_V_DIM)) * (1.0 - lam_init)).astype(x.dtype)
        k_out.append(k.reshape(B, T, ATT_HEADS, 2 * ATT_QK_DIM))
        v_out.append(v)
        y = jnp.concatenate([ya, yb, o.reshape(B, T, ATT_WIDTH)], axis=-1) @ w_out[l]
        x = x + gt1 * y
        h2 = rmsnorm(x, g_norm[l, 1]) * (1 + sc2) + sh2
        x = x + gt2 * peer_ffn(h2, w_pq[l], sub_keys[l], u_tab[l], v_tab[l])
    return x, jnp.stack(k_out), jnp.stack(v_out), jnp.stack(pool_out), jnp.stack(gv_out)


def setup_inputs(seed: int = 0) -> dict:
    key = jax.random.key(seed)
    ks = jax.random.split(key, 32)
    f32 = jnp.float32
    D = D_MODEL

    def nrm(k, shape, s):
        return jax.random.normal(k, shape, f32) * s

    return {
        'x_prompt': nrm(ks[0], (BATCH, SEQ, D), 1.0),
        'x_sample': nrm(ks[1], (DEC_BATCH, DEC_SEQ, D), 1.0),
        'c_prompt': nrm(ks[2], (BATCH, D), 1.0),
        'c_sample': nrm(ks[3], (DEC_BATCH, D), 1.0),
        'cache_k': nrm(ks[4], (DEPTH, DEC_BATCH, PAST_LEN, ATT_HEADS, 2 * ATT_QK_DIM), 1.0),
        'cache_v': nrm(ks[5], (DEPTH, DEC_BATCH, PAST_LEN, ATT_HEADS, ATT_V_DIM), 1.0),
        'state_pool': nrm(ks[6], (DEPTH, DEC_BATCH, POOL_HIST, POOL_WIDTH), 1.0),
        'rel_bias': nrm(ks[7], (REL_BUCKETS, ATT_HEADS), 0.2),
        'w_ada': nrm(ks[8], (DEPTH, D, N_MOD * D), 0.5 * D ** -0.5),
        'b_ada': nrm(ks[9], (DEPTH, N_MOD * D), 0.02),
        'g_norm': 1.0 + nrm(ks[10], (DEPTH, 2, D), 0.05),
        'w_in': nrm(ks[11], (DEPTH, D, IN_WIDTH), D ** -0.5),
        'w_out': nrm(ks[12], (DEPTH, MIX_WIDTH, D), MIX_WIDTH ** -0.5),
        'w_pool': nrm(ks[13], (DEPTH, POOL_GROUPS, POOL_GDIM, POOL_GDIM), POOL_GDIM ** -0.5),
        's_pool': 1.0 + nrm(ks[14], (DEPTH, POOL_WIDTH), 0.1),
        'g_gv': 1.0 + nrm(ks[15], (DEPTH, GMLP_WIDTH), 0.05),
        'w_s': nrm(ks[16], (DEPTH, GMLP_GROUPS, GMLP_CHUNK, GMLP_CHUNK), GMLP_CHUNK ** -0.5),
        'b_s': 1.0 + nrm(ks[17], (DEPTH, GMLP_GROUPS, GMLP_CHUNK), 0.1),
        'g_qk': 1.0 + nrm(ks[18], (DEPTH, 2, ATT_QK_DIM), 0.05),
        'lam_vecs': nrm(ks[19], (DEPTH, 4, ATT_QK_DIM), 0.1),
        'g_sub': 1.0 + nrm(ks[20], (DEPTH, ATT_WIDTH), 0.05),
        'w_pq': nrm(ks[21], (DEPTH, D, PEER_HEADS * PEER_QDIM), D ** -0.5),
        'sub_keys': nrm(ks[22], (DEPTH, PEER_HEADS, 2, PEER_NKEYS, PEER_HALF), PEER_HALF ** -0.5),
        'u_tab': nrm(ks[23], (DEPTH, PEER_EXPERTS, D), D ** -0.5),
        'v_tab': nrm(ks[24], (DEPTH, PEER_EXPERTS, D), PEER_HEADS ** -0.5),
    }


def reference(x_prompt, x_sample, c_prompt, c_sample, cache_k, cache_v, state_pool,
              rel_bias, w_ada, b_ada, g_norm, w_in, w_out, w_pool, s_pool, g_gv, w_s, b_s,
              g_qk, lam_vecs, g_sub, w_pq, sub_keys, u_tab, v_tab):
    y_prompt, k_prompt, v_prompt, pool_prompt, _ = run_trunk(
        x_prompt, c_prompt, None, None, None, rel_bias, w_ada, b_ada, g_norm, w_in, w_out,
        w_pool, s_pool, g_gv, w_s, b_s, g_qk, lam_vecs, g_sub, w_pq, sub_keys, u_tab, v_tab)
    y_sample, k_sample, v_sample, pool_sample, gmlp_v_sample = run_trunk(
        x_sample, c_sample, state_pool, cache_k, cache_v, rel_bias, w_ada, b_ada, g_norm, w_in, w_out,
        w_pool, s_pool, g_gv, w_s, b_s, g_qk, lam_vecs, g_sub, w_pq, sub_keys, u_tab, v_tab)
    return (y_prompt, y_sample, k_prompt, v_prompt, pool_prompt, k_sample, v_sample, pool_sample, gmlp_v_sample)
```

```python
import functools
import math

import jax
import jax.numpy as jnp
from jax import lax
from jax.experimental import pallas as pl
from jax.experimental.pallas import tpu as pltpu

F32 = jnp.float32
BF16 = jnp.bfloat16

EPS = 1e-6
ATT_CHUNK = 64
POOL_WINDOWS = (2, 4, 8, 16)
POOL_CARRY = 16
GMLP_CHUNK = 128
REL_BUCKETS = 32
REL_MAX_DIST = 128
PEER_TOPK = 16
NEG_BIG = -1e30

LANES = 128
SUBLANES = 8
VMEM_LIMIT = 56 * 1024 * 1024


def _cparams(sem):
    return pltpu.CompilerParams(dimension_semantics=sem, vmem_limit_bytes=VMEM_LIMIT)


def _dot(a, b):
    return jnp.dot(a, b, preferred_element_type=F32)


def _idiv_pow2(x, n):
    assert n & (n - 1) == 0, n
    return jnp.right_shift(x, n.bit_length() - 1)


def _dot_nt(a, b):
    return lax.dot_general(a, b, (((1,), (1,)), ((), ())), preferred_element_type=F32)


def _ada_kernel(c_ref, w_ref, b_ref, o_ref):
    c = c_ref[...]
    s = c * jax.nn.sigmoid(c)
    o_ref[...] = _dot(s, w_ref[...]) + b_ref[...]


def _ada_call(c, w, b):
    m, d = c.shape
    n = w.shape[1]
    nb = 1536
    return pl.pallas_call(
        _ada_kernel,
        out_shape=jax.ShapeDtypeStruct((m, n), F32),
        grid=(n // nb,),
        in_specs=[pl.BlockSpec((m, d), lambda j: (0, 0)),
                  pl.BlockSpec((d, nb), lambda j: (0, j)),
                  pl.BlockSpec((1, nb), lambda j: (0, j))],
        out_specs=pl.BlockSpec((m, nb), lambda j: (0, j)),
        compiler_params=_cparams(("arbitrary",)),
        name="ada_mod",
    )(c, w, b.reshape(1, n))


def _rms_rows(x):
    return x * lax.rsqrt(jnp.mean(x * x, axis=-1, keepdims=True) + EPS)


def _group_norm_256(t, p_ref):
    pieces = []
    for s in range(t.shape[1] // 256):
        ts = t[:, s * 256:(s + 1) * 256]
        msq = _dot((ts * ts).astype(BF16), p_ref[...])
        pieces.append(ts * lax.rsqrt(msq + EPS))
    return pieces[0] if len(pieces) == 1 else jnp.concatenate(pieces, axis=-1)


def _k1_kernel(has_res, widths, *refs):
    if has_res:
        x_ref, po_ref, gt_ref = refs[:3]
        refs = refs[3:]
    else:
        x_ref = refs[0]
        refs = refs[1:]
    sh_ref, sc_ref, g_ref, win_ref, p_ref, gv_ref, gq_ref, gk_ref = refs[:8]
    outs = refs[8:]
    x = x_ref[0]
    if has_res:
        x = x + gt_ref[0] * po_ref[0]
        outs[0][0] = x
        outs = outs[1:]
    za_ref, zu_ref, vn_ref, q_ref, k_ref, v_ref = outs
    h = _rms_rows(x) * g_ref[...] * (1.0 + sc_ref[0]) + sh_ref[0]
    z = _dot(h.astype(BF16), win_ref[...])
    wa, wg, wqk, wv = widths
    o = 0
    za_ref[0] = z[:, o:o + wa]; o += wa
    zu_ref[0] = z[:, o:o + wg]; o += wg
    vn_ref[0] = _group_norm_256(z[:, o:o + wg], p_ref) * gv_ref[...]; o += wg
    q_ref[0] = _group_norm_256(z[:, o:o + wqk], p_ref) * gq_ref[...]; o += wqk
    k_ref[0] = _group_norm_256(z[:, o:o + wqk], p_ref) * gk_ref[...]; o += wqk
    v_ref[0] = z[:, o:o + wv]


def _tok_spec(tm, d):
    return pl.BlockSpec((1, tm, d), lambda b, s: (b, s, 0))


def _mod_spec(per_token, tm, d):
    if per_token:
        return pl.BlockSpec((1, tm, d), lambda b, s: (b, s, 0))
    return pl.BlockSpec((1, 1, d), lambda b, s: (b, 0, 0))


def _full_spec(shape):
    nd = len(shape)
    return pl.BlockSpec(shape, lambda b, s: (0,) * nd)


def _k1_call(x, res, sh, sc, g, w_in, p256, gv, gq, gk, widths, tm, per_token):
    bsz, seq, d = x.shape
    wa, wg, wqk, wv = widths
    has_res = res is not None
    ins, specs = [x], [_tok_spec(tm, d)]
    if has_res:
        po, gt = res
        ins += [po, gt]
        specs += [_tok_spec(tm, d), _mod_spec(per_token, tm, d)]
    ins += [sh, sc, g, w_in, p256, gv, gq, gk]
    specs += [_mod_spec(per_token, tm, d), _mod_spec(per_token, tm, d), _full_spec(g.shape),
              _full_spec(w_in.shape), _full_spec(p256.shape), _full_spec(gv.shape),
              _full_spec(gq.shape), _full_spec(gk.shape)]
    out_w = ([d] if has_res else []) + [wa, wg, wg, wqk, wqk, wv]
    return pl.pallas_call(
        functools.partial(_k1_kernel, has_res, widths),
        out_shape=[jax.ShapeDtypeStruct((bsz, seq, w), F32) for w in out_w],
        grid=(bsz, seq // tm),
        in_specs=specs,
        out_specs=[_tok_spec(tm, w) for w in out_w],
        compiler_params=_cparams(("arbitrary", "arbitrary")),
        name="in_proj",
    )(*ins)


def _k2_kernel(pos0, za_ref, hist_ref, zu_ref, vn_ref, wp_ref, sp_ref, ws_ref, bsp_ref,
               ya_ref, yb_ref, carry_ref):
    s = pl.program_id(1)
    tb = za_ref.shape[1]
    w = za_ref.shape[2]
    gw = w // len(POOL_WINDOWS)

    @pl.when(s == 0)
    def _():
        carry_ref[...] = hist_ref[0]

    za = za_ref[0]
    ext = jnp.concatenate([carry_ref[...], za], axis=0)
    carry_ref[...] = za[tb - POOL_CARRY:, :]
    sums = {1: ext}
    d = 1
    while d < max(POOL_WINDOWS):
        sums[2 * d] = sums[d] + pltpu.roll(sums[d], d, axis=0)
        d *= 2
    row = lax.broadcasted_iota(jnp.int32, (tb, w), 0)
    lane_grp = _idiv_pow2(lax.broadcasted_iota(jnp.int32, (tb, w), 1), gw)
    pos1 = (pos0 + s * tb + row + 1).astype(F32)
    means = None
    for gi, win in enumerate(POOL_WINDOWS):
        m = sums[win][POOL_CARRY:, :] / jnp.minimum(float(win), pos1)
        means = m if means is None else jnp.where(lane_grp == gi, m, means)
    dlt = means - za
    ya_ref[0] = _dot(dlt.astype(BF16), wp_ref[...]) * sp_ref[...]

    ll = ws_ref.shape[1]
    r_i = lax.broadcasted_iota(jnp.int32, (ll, ll), 0)
    c_i = lax.broadcasted_iota(jnp.int32, (ll, ll), 1)
    grp = _idiv_pow2(lax.broadcasted_iota(jnp.int32, (ll, w), 1), w // ws_ref.shape[0])
    for c in range(tb // ll):
        vnc = vn_ref[0, c * ll:(c + 1) * ll, :].astype(BF16)
        spv = None
        for gi in range(ws_ref.shape[0]):
            wsg = jnp.where(r_i >= c_i, ws_ref[gi], 0.0).astype(BF16)
            t = _dot(wsg, vnc)
            spv = t if spv is None else jnp.where(grp == gi, t, spv)
        yb_ref[0, c * ll:(c + 1) * ll, :] = zu_ref[0, c * ll:(c + 1) * ll, :] * (spv + bsp_ref[...])


def _k2_call(za, hist16, zu, vn, wp_bd, s_pool, ws, bsp, pos0, tb):
    bsz, seq, w = za.shape
    return pl.pallas_call(
        functools.partial(_k2_kernel, pos0),
        out_shape=[jax.ShapeDtypeStruct((bsz, seq, w), F32)] * 2,
        grid=(bsz, seq // tb),
        in_specs=[_tok_spec(tb, w),
                  pl.BlockSpec((1, POOL_CARRY, w), lambda b, s: (b, 0, 0)),
                  _tok_spec(tb, w), _tok_spec(tb, w),
                  _full_spec(wp_bd.shape), _full_spec(s_pool.shape), _full_spec(ws.shape),
                  _full_spec(bsp.shape)],
        out_specs=[_tok_spec(tb, w)] * 2,
        scratch_shapes=[pltpu.VMEM((POOL_CARRY, w), F32)],
        compiler_params=_cparams(("arbitrary", "arbitrary")),
        name="pool_gmlp",
    )(za, hist16, zu, vn, wp_bd, s_pool, ws, bsp)


def _lambda(lamv_ref, lam_init):
    lv = lamv_ref[...]
    a = jnp.sum(lv[0:1] * lv[1:2], axis=-1, keepdims=True)
    b = jnp.sum(lv[2:3] * lv[3:4], axis=-1, keepdims=True)
    return jnp.exp(a) - jnp.exp(b) + lam_init


def _split_maps(q):
    lane = lax.broadcasted_iota(jnp.int32, q.shape, 1)
    half = q.shape[1] // 2
    return (jnp.where(lane < half, q, 0.0).astype(BF16),
            jnp.where(lane >= half, q, 0.0).astype(BF16))


def _online_update(state, s, vblk):
    m, l, acc = state
    m_new = jnp.maximum(m, jnp.max(s, axis=-1, keepdims=True))
    alpha = jnp.exp(m - m_new)
    p = jnp.exp(s - m_new)
    l_new = alpha * l + jnp.sum(p, axis=-1, keepdims=True)
    acc_new = alpha * acc + _dot(p.astype(BF16), vblk)
    return m_new, l_new, acc_new


def _sub_norm(o0, o1, lam, lam_init, gsub):
    o = o0 - lam * o1
    return _rms_rows(o) * gsub * (1.0 - lam_init)


def _attn_prompt_kernel(lam_init, scale, q_ref, k_ref, v_ref, bias_ref, lamv_ref, gsub_ref, o_ref):
    qb = pl.program_id(2)
    tq = q_ref.shape[1]
    dv = v_ref.shape[2]
    q0, q1 = _split_maps(q_ref[0] * scale)

    def init():
        return (jnp.full((tq, 1), NEG_BIG, F32), jnp.zeros((tq, 1), F32), jnp.zeros((tq, dv), F32))

    def block(kb, bias):
        kblk = k_ref[0, pl.ds(pl.multiple_of(kb * tq, tq), tq), :].astype(BF16)
        vblk = v_ref[0, pl.ds(pl.multiple_of(kb * tq, tq), tq), :].astype(BF16)
        return _dot_nt(q0, kblk) + bias, _dot_nt(q1, kblk) + bias, vblk

    def body(kb, carry):
        st0, st1 = carry
        bias = bias_ref[0, jnp.minimum(qb - kb, 2)]
        s0, s1, vblk = block(kb, bias)
        return _online_update(st0, s0, vblk), _online_update(st1, s1, vblk)

    st0, st1 = lax.fori_loop(0, qb, body, (init(), init()))
    s0, s1, vblk = block(qb, bias_ref[0, 0])
    qi = _idiv_pow2(lax.broadcasted_iota(jnp.int32, (tq, tq), 0), ATT_CHUNK)
    ki = _idiv_pow2(lax.broadcasted_iota(jnp.int32, (tq, tq), 1), ATT_CHUNK)
    vis = ki <= qi
    st0 = _online_update(st0, jnp.where(vis, s0, NEG_BIG), vblk)
    st1 = _online_update(st1, jnp.where(vis, s1, NEG_BIG), vblk)
    lam = _lambda(lamv_ref, lam_init)
    o_ref[0] = _sub_norm(st0[2] / st0[1], st1[2] / st1[1], lam, lam_init, gsub_ref[0])


def _attn_prompt_call(qn, kn, v, bias_tiles, lamv, gsub, lam_init, heads, tq):
    bsz, seq, _ = qn.shape
    dh = qn.shape[2] // heads
    dv = v.shape[2] // heads
    scale = float((dh // 2) ** -0.5)
    return pl.pallas_call(
        functools.partial(_attn_prompt_kernel, lam_init, scale),
        out_shape=jax.ShapeDtypeStruct((bsz, seq, heads * dv), F32),
        grid=(bsz, heads, seq // tq),
        in_specs=[pl.BlockSpec((1, tq, dh), lambda b, h, i: (b, i, h)),
                  pl.BlockSpec((1, seq, dh), lambda b, h, i: (b, 0, h)),
                  pl.BlockSpec((1, seq, dv), lambda b, h, i: (b, 0, h)),
                  pl.BlockSpec((1, 3, tq, tq), lambda b, h, i: (h, 0, 0, 0)),
                  pl.BlockSpec(lamv.shape, lambda b, h, i: (0, 0)),
                  pl.BlockSpec((1, 1, dv), lambda b, h, i: (h, 0, 0))],
        out_specs=pl.BlockSpec((1, tq, dv), lambda b, h, i: (b, i, h)),
        compiler_params=_cparams(("arbitrary", "arbitrary", "arbitrary")),
        name="attn_prompt",
    )(qn, kn, v, bias_tiles, lamv, gsub)


def _attn_sample_kernel(lam_init, scale, heads, q_ref, kc_ref, vc_ref, kn_ref, vn_ref,
                        bc_ref, bn_ref, lamv_ref, gsub_ref, o_ref):
    t = q_ref.shape[1]
    dh = q_ref.shape[2] // heads
    dv = vc_ref.shape[2] // heads
    lam = _lambda(lamv_ref, lam_init)
    pad = bn_ref.shape[2] - t
    for h in range(heads):
        q0, q1 = _split_maps(q_ref[0, :, h * dh:(h + 1) * dh] * scale)
        kc = kc_ref[0, :, h * dh:(h + 1) * dh].astype(BF16)
        vc = vc_ref[0, :, h * dv:(h + 1) * dv].astype(BF16)
        kn = jnp.concatenate([kn_ref[0, :, h * dh:(h + 1) * dh], jnp.zeros((pad, dh), F32)], axis=0).astype(BF16)
        vn = jnp.concatenate([vn_ref[0, :, h * dv:(h + 1) * dv], jnp.zeros((pad, dv), F32)], axis=0).astype(BF16)
        outs = []
        for qm in (q0, q1):
            sc = _dot_nt(qm, kc) + bc_ref[h]
            sn = _dot_nt(qm, kn) + bn_ref[h]
            m = jnp.maximum(jnp.max(sc, axis=-1, keepdims=True), jnp.max(sn, axis=-1, keepdims=True))
            pc = jnp.exp(sc - m)
            pn = jnp.exp(sn - m)
            l = jnp.sum(pc, axis=-1, keepdims=True) + jnp.sum(pn, axis=-1, keepdims=True)
            outs.append((_dot(pc.astype(BF16), vc) + _dot(pn.astype(BF16), vn)) / l)
        o_ref[0, :, h * dv:(h + 1) * dv] = _sub_norm(outs[0], outs[1], lam, lam_init, gsub_ref[h])


def _attn_sample_call(qn, kc, vc, kn, vn, bias_c, bias_n, lamv, gsub, lam_init, heads):
    bsz, t, wq = qn.shape
    past = kc.shape[1]
    wv = vc.shape[2]
    dh = wq // heads
    scale = float((dh // 2) ** -0.5)
    row = lambda shape: pl.BlockSpec((1,) + shape[1:], lambda b: (b, 0, 0))
    full = lambda shape: pl.BlockSpec(shape, lambda b: (0,) * len(shape))
    return pl.pallas_call(
        functools.partial(_attn_sample_kernel, lam_init, scale, heads),
        out_shape=jax.ShapeDtypeStruct((bsz, t, wv), F32),
        grid=(bsz,),
        in_specs=[row(qn.shape), row(kc.shape), row(vc.shape), row(kn.shape), row(vn.shape),
                  full(bias_c.shape), full(bias_n.shape), full(lamv.shape), full(gsub.shape)],
        out_specs=row((bsz, t, wv)),
        compiler_params=_cparams(("arbitrary",)),
        name="attn_sample",
    )(qn, kc, vc, kn, vn, bias_c, bias_n, lamv, gsub)


def _k3_kernel(x_ref, ya_ref, yb_ref, o_ref, gt_ref, sh_ref, sc_ref, g_ref, wout_ref, wpq_ref, keys_ref,
               x1_ref, st_ref, h2t_ref):
    wa = ya_ref.shape[2]
    wb = yb_ref.shape[2]
    y = (_dot(ya_ref[0].astype(BF16), wout_ref[0:wa, :])
         + _dot(yb_ref[0].astype(BF16), wout_ref[wa:wa + wb, :])
         + _dot(o_ref[0].astype(BF16), wout_ref[wa + wb:, :]))
    x1 = x_ref[0] + gt_ref[0] * y
    x1_ref[0] = x1
    h2 = _rms_rows(x1) * g_ref[...] * (1.0 + sc_ref[0]) + sh_ref[0]
    h2t_ref[...] = h2.T.astype(BF16)
    qp = _dot(h2.astype(BF16), wpq_ref[...])
    nk = keys_ref.shape[1]
    kd = keys_ref.shape[2]
    for r in range(keys_ref.shape[0]):
        st_ref[r * nk:(r + 1) * nk, :] = _dot_nt(keys_ref[r], qp[:, r * kd:(r + 1) * kd].astype(BF16))


def _k3_call(x, ya, yb, o, gt, sh, sc, g, w_out, w_pq, keys, tm, per_token):
    bsz, seq, d = x.shape
    n = bsz * seq
    ns = seq // tm
    nrow = keys.shape[0] * keys.shape[1]
    flat = lambda b, s: (0, b * ns + s)
    return pl.pallas_call(
        _k3_kernel,
        out_shape=[jax.ShapeDtypeStruct((bsz, seq, d), F32),
                   jax.ShapeDtypeStruct((nrow, n), F32),
                   jax.ShapeDtypeStruct((d, n), BF16)],
        grid=(bsz, ns),
        in_specs=[_tok_spec(tm, d), _tok_spec(tm, ya.shape[2]), _tok_spec(tm, yb.shape[2]),
                  _tok_spec(tm, o.shape[2]),
                  _mod_spec(per_token, tm, d), _mod_spec(per_token, tm, d), _mod_spec(per_token, tm, d),
                  _full_spec(g.shape), _full_spec(w_out.shape), _full_spec(w_pq.shape),
                  _full_spec(keys.shape)],
        out_specs=[_tok_spec(tm, d), pl.BlockSpec((nrow, tm), flat), pl.BlockSpec((d, tm), flat)],
        compiler_params=_cparams(("arbitrary", "arbitrary")),
        name="out_proj_peer_scores",
    )(x, ya, yb, o, gt, sh, sc, g, w_out, w_pq, keys)


def _extract_top(cur, rounds):
    rows = cur.shape[0]
    ridx = lax.broadcasted_iota(jnp.int32, cur.shape, 0).astype(F32)
    vals = []
    for _ in range(rounds):
        m = jnp.max(cur, axis=0, keepdims=True)
        first = jnp.min(jnp.where(cur == m, ridx, float(rows)), axis=0, keepdims=True)
        cur = jnp.where(ridx == first, -jnp.inf, cur)
        vals.append(m)
    return vals


def _k4_kernel(st_ref, stats_ref, top_ref):
    nlist = top_ref.shape[0]
    nk = st_ref.shape[0] // nlist
    k = PEER_TOPK

    def lists(r, _):
        cur = st_ref[pl.ds(pl.multiple_of(r * nk, nk), nk), :]
        top_ref[r] = jnp.concatenate(_extract_top(cur, k), axis=0)
        return 0

    lax.fori_loop(0, nlist, lists, 0)

    def heads(h, _):
        a = top_ref[2 * h]
        b = top_ref[2 * h + 1]
        pieces = [a[0:1] + b]
        for p in range(1, SUBLANES):
            pieces.append(a[p:p + 1] + b[0:SUBLANES])
        pieces.append(a[SUBLANES:] + b[0:1])
        tv = _extract_top(jnp.concatenate(pieces, axis=0), k)
        z = jnp.ones_like(tv[0])
        for r in range(1, k):
            z = z + jnp.exp(tv[r] - tv[0])
        zero = jnp.zeros_like(z)
        stats_ref[h] = jnp.concatenate([tv[k - 1], a[0:1], b[0:1], 1.0 / z, zero, zero, zero, zero], axis=0)
        return 0

    lax.fori_loop(0, nlist // 2, heads, 0)


def _k4_call(st, nlist, tt):
    nrow, n = st.shape
    nh = nlist // 2
    return pl.pallas_call(
        _k4_kernel,
        out_shape=jax.ShapeDtypeStruct((nh, SUBLANES, n), F32),
        grid=(n // tt,),
        in_specs=[pl.BlockSpec((nrow, tt), lambda i: (0, i))],
        out_specs=pl.BlockSpec((nh, SUBLANES, tt), lambda i: (0, 0, i)),
        scratch_shapes=[pltpu.VMEM((nlist, PEER_TOPK, tt), F32)],
        compiler_params=_cparams(("arbitrary",)),
        name="peer_topk_stats",
    )(st)


def _k5_kernel(nheads, nk, ht_ref, st_ref, stats_ref, u_ref, vt_ref, out_ref,
               e1_ref, e2_ref, g_ref, acc_ref):
    e = pl.program_id(1)
    ne = pl.num_programs(1)
    eb = u_ref.shape[0]
    tt = ht_ref.shape[1]
    sqrt_half = math.sqrt(0.5)

    @pl.when(e == 0)
    def _():
        acc_ref[...] = jnp.zeros_like(acc_ref)
        for h in range(nheads):
            s1 = st_ref[(2 * h) * nk:(2 * h + 1) * nk, :]
            s2 = st_ref[(2 * h + 1) * nk:(2 * h + 2) * nk, :]
            e1_ref[h * nk:(h + 1) * nk, :] = jnp.exp(s1 - stats_ref[h, 1:2, :])
            e2_ref[h * nk:(h + 1) * nk, :] = jnp.exp(s2 - stats_ref[h, 2:3, :]) * stats_ref[h, 3:4, :]

    at = _dot(u_ref[...], ht_ref[...])
    ni = eb // nk
    assert ni == SUBLANES, (eb, nk)
    i0 = pl.multiple_of(e * ni, ni)
    for c in range(tt // LANES):
        cs = slice(c * LANES, (c + 1) * LANES)
        s1g = [st_ref[pl.ds(2 * h * nk + i0, ni), cs] for h in range(nheads)]
        e1g = [e1_ref[pl.ds(h * nk + i0, ni), cs] for h in range(nheads)]
        for ii in range(ni):
            w = jnp.zeros((nk, LANES), F32)
            for h in range(nheads):
                s2 = st_ref[(2 * h + 1) * nk:(2 * h + 2) * nk, cs]
                sel = (s1g[h][ii:ii + 1] + s2) >= stats_ref[h, 0:1, cs]
                w = w + e1g[h][ii:ii + 1] * jnp.where(sel, e2_ref[h * nk:(h + 1) * nk, cs], 0.0)
            a = at[ii * nk:(ii + 1) * nk, cs]
            gelu = 0.5 * a * (1.0 + lax.erf(a * sqrt_half))
            g_ref[ii * nk:(ii + 1) * nk, cs] = (w * gelu).astype(BF16)
    acc_ref[...] += _dot(vt_ref[...], g_ref[...])

    @pl.when(e == ne - 1)
    def _():
        out_ref[...] = acc_ref[...].T


def _k5_call(h2t, st, stats, u_bf, vt_bf, nheads, nk, tt, eb):
    d, n = h2t.shape
    ne = u_bf.shape[0] // eb
    return pl.pallas_call(
        functools.partial(_k5_kernel, nheads, nk),
        out_shape=jax.ShapeDtypeStruct((n, d), F32),
        grid=(n // tt, ne),
        in_specs=[pl.BlockSpec((d, tt), lambda i, e: (0, i)),
                  pl.BlockSpec((st.shape[0], tt), lambda i, e: (0, i)),
                  pl.BlockSpec((nheads, SUBLANES, tt), lambda i, e: (0, 0, i)),
                  pl.BlockSpec((eb, d), lambda i, e: (e, 0)),
                  pl.BlockSpec((d, eb), lambda i, e: (0, e))],
        out_specs=pl.BlockSpec((tt, d), lambda i, e: (i, 0)),
        scratch_shapes=[pltpu.VMEM((nheads * nk, tt), F32), pltpu.VMEM((nheads * nk, tt), F32),
                        pltpu.VMEM((eb, tt), BF16), pltpu.VMEM((d, tt), F32)],
        compiler_params=_cparams(("arbitrary", "arbitrary")),
        name="peer_dense",
    )(h2t, st, stats, u_bf, vt_bf)


def _res_kernel(x_ref, po_ref, gt_ref, o_ref):
    o_ref[0] = x_ref[0] + gt_ref[0] * po_ref[0]


def _res_call(x, po, gt, tm, per_token):
    bsz, seq, d = x.shape
    return pl.pallas_call(
        _res_kernel,
        out_shape=jax.ShapeDtypeStruct((bsz, seq, d), F32),
        grid=(bsz, seq // tm),
        in_specs=[_tok_spec(tm, d), _tok_spec(tm, d), _mod_spec(per_token, tm, d)],
        out_specs=_tok_spec(tm, d),
        compiler_params=_cparams(("arbitrary", "arbitrary")),
        name="final_residual",
    )(x, po, gt)


def _t5_bucket(rel):
    nb = REL_BUCKETS // 2
    max_exact = nb // 2
    side = jnp.where(rel > 0, nb, 0)
    n = jnp.abs(rel)
    nf = jnp.maximum(n, 1).astype(jnp.float32)
    large = max_exact + (jnp.log(nf / max_exact) / math.log(REL_MAX_DIST / max_exact)
                         * (nb - max_exact)).astype(jnp.int32)
    large = jnp.minimum(large, nb - 1)
    return side + jnp.where(n < max_exact, n, large)


def _bias_table(rel_bias, q_pos, k_pos):
    b = rel_bias.astype(F32)[_t5_bucket(k_pos[None, :] - q_pos[:, None])]
    return jnp.transpose(b, (2, 0, 1))


def _block_diag(w):
    g, a, b = w.shape
    eye = jnp.eye(g, dtype=w.dtype)
    return (eye[:, None, :, None] * w[:, :, None, :]).reshape(g * a, g * b)


def _trunk(x, c_mod, per_token, sample, layer_w, rel_bias, tm, tq, tt, eb):
    bsz, seq, d = x.shape
    depth = len(layer_w)
    k_out, v_out, pool_out, gv_out = [], [], [], []
    tok = (lambda t: t.reshape(1, bsz * seq, t.shape[-1])) if per_token else (lambda t: t)
    seqv = lambda t: t.reshape(bsz, seq, t.shape[-1])
    xt = tok(x)
    res = None
    for l in range(depth):
        w = layer_w[l]
        sh1, sc1, gt1, sh2, sc2, gt2 = c_mod[l]
        heads = w["heads"]
        widths = w["widths"]
        outs = _k1_call(xt, res, sh1, sc1, w["g1"], w["w_in"], w["p256"], w["g_gv"],
                        w["gq"], w["gk"], widths, tm, per_token)
        if res is not None:
            xt, outs = outs[0], outs[1:]
        za, zu, vn, qn, kn, v = [seqv(t) for t in outs]
        if sample is None:
            hist = jnp.zeros((bsz, POOL_CARRY, za.shape[2]), F32)
            ya, yb = _k2_call(za, hist, zu, vn, w["wp_bd"], w["s_pool"], w["w_s"], w["bsp"], 0, 2 * GMLP_CHUNK)
            pool_out.append(za[:, seq - (POOL_CARRY - 1):])
        else:
            state_pool, cache_k, cache_v = sample
            hist = jnp.pad(state_pool[l], ((0, 0), (1, 0), (0, 0)))
            padr = ((0, 0), (0, GMLP_CHUNK - seq), (0, 0))
            ya, yb = _k2_call(jnp.pad(za, padr), hist, jnp.pad(zu, padr), jnp.pad(vn, padr), w["wp_bd"],
                              w["s_pool"], w["w_s"], w["bsp"], cache_k.shape[2], GMLP_CHUNK)
            ya, yb = ya[:, :seq], yb[:, :seq]
            pool_out.append(jnp.concatenate([state_pool[l], za], axis=1)[:, -(POOL_CARRY - 1):])
            gv_out.append(vn)
        lam_init = 0.8 - 0.6 * math.exp(-0.3 * l)
        if sample is None:
            pos = jnp.arange(tq)
            tiles = jnp.stack([_bias_table(rel_bias, pos, pos - dlt * tq) for dlt in range(3)], axis=1)
            o = _attn_prompt_call(qn, kn, v, tiles, w["lamv"], w["g_sub"], lam_init, heads, tq)
        else:
            past = cache_k.shape[2]
            q_pos = past + jnp.arange(seq)
            bias_c = _bias_table(rel_bias, q_pos, jnp.arange(past))
            bias_n = _bias_table(rel_bias, q_pos, q_pos)
            bias_n = jnp.pad(bias_n, ((0, 0), (0, 0), (0, LANES - seq)), constant_values=NEG_BIG)
            o = _attn_sample_call(qn, cache_k[l].reshape(bsz, past, -1), cache_v[l].reshape(bsz, past, -1),
                                  kn, v, bias_c, bias_n, w["lamv"], w["g_sub"], lam_init, heads)
        k_out.append(kn.reshape(bsz, seq, heads, -1))
        v_out.append(v.reshape(bsz, seq, heads, -1))
        xt, st, h2t = _k3_call(xt, tok(ya), tok(yb), tok(o), gt1, sh2, sc2, w["g2"], w["w_out"], w["w_pq"],
                               w["keys"], tm, per_token)
        nlist = w["keys"].shape[0]
        nk = w["keys"].shape[1]
        stats = _k4_call(st, nlist, LANES)
        po = _k5_call(h2t, st, stats, w["u"], w["vt"], nlist // 2, nk, tt, eb)
        res = (po.reshape(xt.shape), gt2)
    y = _res_call(xt, res[0], res[1], tm, per_token).reshape(bsz, seq, d)
    return y, jnp.stack(k_out), jnp.stack(v_out), jnp.stack(pool_out), (jnp.stack(gv_out) if gv_out else None)


def kernel(x_prompt, x_sample, c_prompt, c_sample, cache_k, cache_v, state_pool, rel_bias, w_ada, b_ada,
           g_norm, w_in, w_out, w_pool, s_pool, g_gv, w_s, b_s, g_qk, lam_vecs, g_sub, w_pq, sub_keys,
           u_tab, v_tab):
    depth = w_in.shape[0]
    d = x_prompt.shape[2]
    bp, sp = x_prompt.shape[:2]
    bs, ss = x_sample.shape[:2]
    heads = cache_k.shape[3]
    dqk = cache_k.shape[4]
    dv = cache_v.shape[4]
    wa = w_pool.shape[1] * w_pool.shape[2]
    wg = g_gv.shape[1]
    widths = (wa, wg, heads * dqk, heads * dv)
    grp = dqk // 2

    p256 = _block_diag(jnp.full((256 // grp, grp, grp), 1.0 / grp, F32)).astype(BF16)
    c_all = jnp.concatenate([c_prompt, c_sample], axis=0)
    layer_w, mods_p, mods_s = [], [], []
    for l in range(depth):
        mod = _ada_call(c_all, w_ada[l], b_ada[l])
        six = jnp.split(mod, 6, axis=-1)
        mods_p.append([m[:bp, None, :] for m in six])
        mods_s.append([jnp.broadcast_to(m[bp:, None, :], (bs, ss, d)).reshape(1, bs * ss, d) for m in six])
        nh, two, nk, kd = sub_keys.shape[1:]
        layer_w.append(dict(
            heads=heads, widths=widths,
            g1=g_norm[l, 0].reshape(1, d), g2=g_norm[l, 1].reshape(1, d),
            w_in=w_in[l].astype(BF16), w_out=w_out[l].astype(BF16), w_pq=w_pq[l].astype(BF16),
            p256=p256, g_gv=g_gv[l].reshape(1, wg),
            gq=jnp.tile(g_qk[l, 0], heads * dqk // grp).reshape(1, heads * dqk),
            gk=jnp.tile(g_qk[l, 1], heads * dqk // grp).reshape(1, heads * dqk),
            wp_bd=_block_diag(w_pool[l]).astype(BF16), s_pool=s_pool[l].reshape(1, wa),
            w_s=w_s[l], bsp=jnp.repeat(b_s[l].T, wg // b_s.shape[1], axis=1),
            lamv=lam_vecs[l], g_sub=g_sub[l].reshape(heads, 1, dv),
            keys=sub_keys[l].reshape(nh * two, nk, kd).astype(BF16),
            u=u_tab[l].astype(BF16), vt=v_tab[l].T.astype(BF16)))

    tm = min(256, sp)
    y_p, k_p, v_p, pool_p, _ = _trunk(x_prompt, mods_p, False, None, layer_w, rel_bias,
                                      tm=tm, tq=min(256, sp), tt=min(512, bp * sp), eb=SUBLANES * sub_keys.shape[3])
    y_s, k_s, v_s, pool_s, gv_s = _trunk(x_sample, mods_s, True, (state_pool, cache_k, cache_v), layer_w,
                                         rel_bias, tm=min(256, bs * ss), tq=None, tt=min(512, bs * ss), eb=SUBLANES * sub_keys.shape[3])
    return (y_p, y_s, k_p, v_p, pool_p, k_s, v_s, pool_s, gv_s)
```

```python
import functools
import math

import jax
import jax.numpy as jnp
from jax import lax
from jax.experimental import pallas as pl
from jax.experimental.pallas import tpu as pltpu

F32 = jnp.float32
BF16 = jnp.bfloat16

EPS = 1e-6
ATT_CHUNK = 64
POOL_WINDOWS = (2, 4, 8, 16)
POOL_CARRY = 16
GMLP_CHUNK = 128
REL_BUCKETS = 32
REL_MAX_DIST = 128
PEER_TOPK = 16
NEG_BIG = -1e30

LANES = 128
SUBLANES = 8
VMEM_LIMIT = 56 * 1024 * 1024


def _cparams(sem, flags=None):
    return pltpu.CompilerParams(dimension_semantics=sem, vmem_limit_bytes=VMEM_LIMIT, flags=flags)


def _dot(a, b):
    return jnp.dot(a, b, preferred_element_type=F32)


def _idiv_pow2(x, n):
    assert n & (n - 1) == 0, n
    return jnp.right_shift(x, n.bit_length() - 1)


def _dot_nt(a, b):
    return lax.dot_general(a, b, (((1,), (1,)), ((), ())), preferred_element_type=F32)


def _ada_kernel(c_ref, w_ref, b_ref, o_ref):
    c = c_ref[...]
    s = c * jax.nn.sigmoid(c)
    o_ref[...] = _dot(s, w_ref[...]) + b_ref[...]


def _ada_call(c, w, b):
    m, d = c.shape
    n = w.shape[1]
    nb = 1536
    return pl.pallas_call(
        _ada_kernel,
        out_shape=jax.ShapeDtypeStruct((m, n), F32),
        grid=(n // nb,),
        in_specs=[pl.BlockSpec((m, d), lambda j: (0, 0)),
                  pl.BlockSpec((d, nb), lambda j: (0, j)),
                  pl.BlockSpec((1, nb), lambda j: (0, j))],
        out_specs=pl.BlockSpec((m, nb), lambda j: (0, j)),
        compiler_params=_cparams(("arbitrary",)),
        name="ada_mod",
    )(c, w, b.reshape(1, n))


def _rms_rows(x):
    return x * lax.rsqrt(jnp.mean(x * x, axis=-1, keepdims=True) + EPS)


def _group_norm_256(t, p_ref):
    pieces = []
    for s in range(t.shape[1] // 256):
        ts = t[:, s * 256:(s + 1) * 256]
        msq = _dot((ts * ts).astype(BF16), p_ref[...])
        pieces.append(ts * lax.rsqrt(msq + EPS))
    return pieces[0] if len(pieces) == 1 else jnp.concatenate(pieces, axis=-1)


def _k1_kernel(has_res, widths, *refs):
    if has_res:
        x_ref, po_ref, gt_ref = refs[:3]
        refs = refs[3:]
    else:
        x_ref = refs[0]
        refs = refs[1:]
    sh_ref, sc_ref, g_ref, win_ref, p_ref, gv_ref, gq_ref, gk_ref = refs[:8]
    outs = refs[8:]
    x = x_ref[0]
    if has_res:
        x = x + gt_ref[0] * po_ref[0]
        outs[0][0] = x
        outs = outs[1:]
    za_ref, zu_ref, vn_ref, q_ref, k_ref, v_ref = outs
    h = _rms_rows(x) * g_ref[...] * (1.0 + sc_ref[0]) + sh_ref[0]
    z = _dot(h.astype(BF16), win_ref[...])
    wa, wg, wqk, wv = widths
    o = 0
    za_ref[0] = z[:, o:o + wa]; o += wa
    zu_ref[0] = z[:, o:o + wg]; o += wg
    vn_ref[0] = _group_norm_256(z[:, o:o + wg], p_ref) * gv_ref[...]; o += wg
    q_ref[0] = _group_norm_256(z[:, o:o + wqk], p_ref) * gq_ref[...]; o += wqk
    k_ref[0] = _group_norm_256(z[:, o:o + wqk], p_ref) * gk_ref[...]; o += wqk
    v_ref[0] = z[:, o:o + wv]


def _tok_spec(tm, d):
    return pl.BlockSpec((1, tm, d), lambda b, s: (b, s, 0))


def _mod_spec(per_token, tm, d):
    if per_token:
        return pl.BlockSpec((1, tm, d), lambda b, s: (b, s, 0))
    return pl.BlockSpec((1, 1, d), lambda b, s: (b, 0, 0))


def _full_spec(shape):
    nd = len(shape)
    return pl.BlockSpec(shape, lambda b, s: (0,) * nd)


def _k1_call(x, res, sh, sc, g, w_in, p256, gv, gq, gk, widths, tm, per_token):
    bsz, seq, d = x.shape
    wa, wg, wqk, wv = widths
    has_res = res is not None
    ins, specs = [x], [_tok_spec(tm, d)]
    if has_res:
        po, gt = res
        ins += [po, gt]
        specs += [_tok_spec(tm, d), _mod_spec(per_token, tm, d)]
    ins += [sh, sc, g, w_in, p256, gv, gq, gk]
    specs += [_mod_spec(per_token, tm, d), _mod_spec(per_token, tm, d), _full_spec(g.shape),
              _full_spec(w_in.shape), _full_spec(p256.shape), _full_spec(gv.shape),
              _full_spec(gq.shape), _full_spec(gk.shape)]
    out_w = ([d] if has_res else []) + [wa, wg, wg, wqk, wqk, wv]
    return pl.pallas_call(
        functools.partial(_k1_kernel, has_res, widths),
        out_shape=[jax.ShapeDtypeStruct((bsz, seq, w), F32) for w in out_w],
        grid=(bsz, seq // tm),
        in_specs=specs,
        out_specs=[_tok_spec(tm, w) for w in out_w],
        compiler_params=_cparams(("arbitrary", "arbitrary")),
        name="in_proj",
    )(*ins)


def _k2_kernel(pos0, za_ref, hist_ref, zu_ref, vn_ref, wp_ref, sp_ref, ws_ref, bsp_ref,
               ya_ref, yb_ref, carry_ref):
    s = pl.program_id(1)
    tb = za_ref.shape[1]
    w = za_ref.shape[2]
    gw = w // len(POOL_WINDOWS)

    @pl.when(s == 0)
    def _():
        carry_ref[...] = hist_ref[0]

    za = za_ref[0]
    ext = jnp.concatenate([carry_ref[...], za], axis=0)
    carry_ref[...] = za[tb - POOL_CARRY:, :]
    sums = {1: ext}
    d = 1
    while d < max(POOL_WINDOWS):
        sums[2 * d] = sums[d] + pltpu.roll(sums[d], d, axis=0)
        d *= 2
    row = lax.broadcasted_iota(jnp.int32, (tb, w), 0)
    lane_grp = _idiv_pow2(lax.broadcasted_iota(jnp.int32, (tb, w), 1), gw)
    pos1 = (pos0 + s * tb + row + 1).astype(F32)
    means = None
    for gi, win in enumerate(POOL_WINDOWS):
        m = sums[win][POOL_CARRY:, :] / jnp.minimum(float(win), pos1)
        means = m if means is None else jnp.where(lane_grp == gi, m, means)
    dlt = means - za
    ya_ref[0] = _dot(dlt.astype(BF16), wp_ref[...]) * sp_ref[...]

    ll = ws_ref.shape[1]
    r_i = lax.broadcasted_iota(jnp.int32, (ll, ll), 0)
    c_i = lax.broadcasted_iota(jnp.int32, (ll, ll), 1)
    grp = _idiv_pow2(lax.broadcasted_iota(jnp.int32, (ll, w), 1), w // ws_ref.shape[0])
    for c in range(tb // ll):
        vnc = vn_ref[0, c * ll:(c + 1) * ll, :].astype(BF16)
        spv = None
        for gi in range(ws_ref.shape[0]):
            wsg = jnp.where(r_i >= c_i, ws_ref[gi], 0.0).astype(BF16)
            t = _dot(wsg, vnc)
            spv = t if spv is None else jnp.where(grp == gi, t, spv)
        yb_ref[0, c * ll:(c + 1) * ll, :] = zu_ref[0, c * ll:(c + 1) * ll, :] * (spv + bsp_ref[...])


def _k2_call(za, hist16, zu, vn, wp_bd, s_pool, ws, bsp, pos0, tb):
    bsz, seq, w = za.shape
    return pl.pallas_call(
        functools.partial(_k2_kernel, pos0),
        out_shape=[jax.ShapeDtypeStruct((bsz, seq, w), F32)] * 2,
        grid=(bsz, seq // tb),
        in_specs=[_tok_spec(tb, w),
                  pl.BlockSpec((1, POOL_CARRY, w), lambda b, s: (b, 0, 0)),
                  _tok_spec(tb, w), _tok_spec(tb, w),
                  _full_spec(wp_bd.shape), _full_spec(s_pool.shape), _full_spec(ws.shape),
                  _full_spec(bsp.shape)],
        out_specs=[_tok_spec(tb, w)] * 2,
        scratch_shapes=[pltpu.VMEM((POOL_CARRY, w), F32)],
        compiler_params=_cparams(("arbitrary", "arbitrary")),
        name="pool_gmlp",
    )(za, hist16, zu, vn, wp_bd, s_pool, ws, bsp)


def _lambda(lamv_ref, lam_init):
    lv = lamv_ref[...]
    a = jnp.sum(lv[0:1] * lv[1:2], axis=-1, keepdims=True)
    b = jnp.sum(lv[2:3] * lv[3:4], axis=-1, keepdims=True)
    return jnp.exp(a) - jnp.exp(b) + lam_init


def _split_maps(q):
    lane = lax.broadcasted_iota(jnp.int32, q.shape, 1)
    half = q.shape[1] // 2
    return (jnp.where(lane < half, q, 0.0).astype(BF16),
            jnp.where(lane >= half, q, 0.0).astype(BF16))


def _online_update(state, s, vblk):
    m, l, acc = state
    m_new = jnp.maximum(m, jnp.max(s, axis=-1, keepdims=True))
    alpha = jnp.exp(m - m_new)
    p = jnp.exp(s - m_new)
    l_new = alpha * l + jnp.sum(p, axis=-1, keepdims=True)
    acc_new = alpha * acc + _dot(p.astype(BF16), vblk)
    return m_new, l_new, acc_new


def _sub_norm(o0, o1, lam, lam_init, gsub):
    o = o0 - lam * o1
    return _rms_rows(o) * gsub * (1.0 - lam_init)


def _attn_prompt_kernel(lam_init, scale, q_ref, k_ref, v_ref, bias_ref, lamv_ref, gsub_ref, o_ref):
    qb = pl.program_id(2)
    tq = q_ref.shape[1]
    dv = v_ref.shape[2]
    q0, q1 = _split_maps(q_ref[0] * scale)

    def init():
        return (jnp.full((tq, 1), NEG_BIG, F32), jnp.zeros((tq, 1), F32), jnp.zeros((tq, dv), F32))

    def block(kb, bias):
        kblk = k_ref[0, pl.ds(pl.multiple_of(kb * tq, tq), tq), :].astype(BF16)
        vblk = v_ref[0, pl.ds(pl.multiple_of(kb * tq, tq), tq), :].astype(BF16)
        return _dot_nt(q0, kblk) + bias, _dot_nt(q1, kblk) + bias, vblk

    def body(kb, carry):
        st0, st1 = carry
        bias = bias_ref[0, jnp.minimum(qb - kb, 2)]
        s0, s1, vblk = block(kb, bias)
        return _online_update(st0, s0, vblk), _online_update(st1, s1, vblk)

    st0, st1 = lax.fori_loop(0, qb, body, (init(), init()))
    s0, s1, vblk = block(qb, bias_ref[0, 0])
    qi = _idiv_pow2(lax.broadcasted_iota(jnp.int32, (tq, tq), 0), ATT_CHUNK)
    ki = _idiv_pow2(lax.broadcasted_iota(jnp.int32, (tq, tq), 1), ATT_CHUNK)
    vis = ki <= qi
    st0 = _online_update(st0, jnp.where(vis, s0, NEG_BIG), vblk)
    st1 = _online_update(st1, jnp.where(vis, s1, NEG_BIG), vblk)
    lam = _lambda(lamv_ref, lam_init)
    o_ref[0] = _sub_norm(st0[2] / st0[1], st1[2] / st1[1], lam, lam_init, gsub_ref[0])


def _attn_prompt_call(qn, kn, v, bias_tiles, lamv, gsub, lam_init, heads, tq):
    bsz, seq, _ = qn.shape
    dh = qn.shape[2] // heads
    dv = v.shape[2] // heads
    scale = float((dh // 2) ** -0.5)
    return pl.pallas_call(
        functools.partial(_attn_prompt_kernel, lam_init, scale),
        out_shape=jax.ShapeDtypeStruct((bsz, seq, heads * dv), F32),
        grid=(bsz, heads, seq // tq),
        in_specs=[pl.BlockSpec((1, tq, dh), lambda b, h, i: (b, i, h)),
                  pl.BlockSpec((1, seq, dh), lambda b, h, i: (b, 0, h)),
                  pl.BlockSpec((1, seq, dv), lambda b, h, i: (b, 0, h)),
                  pl.BlockSpec((1, 3, tq, tq), lambda b, h, i: (h, 0, 0, 0)),
                  pl.BlockSpec(lamv.shape, lambda b, h, i: (0, 0)),
                  pl.BlockSpec((1, 1, dv), lambda b, h, i: (h, 0, 0))],
        out_specs=pl.BlockSpec((1, tq, dv), lambda b, h, i: (b, i, h)),
        compiler_params=_cparams(("arbitrary", "arbitrary", "arbitrary")),
        name="attn_prompt",
    )(qn, kn, v, bias_tiles, lamv, gsub)


def _attn_sample_kernel(lam_init, scale, heads, q_ref, kc_ref, vc_ref, kn_ref, vn_ref,
                        bc_ref, bn_ref, lamv_ref, gsub_ref, o_ref):
    t = q_ref.shape[1]
    dh = q_ref.shape[2] // heads
    dv = vc_ref.shape[2] // heads
    lam = _lambda(lamv_ref, lam_init)
    pad = bn_ref.shape[2] - t
    for h in range(heads):
        q0, q1 = _split_maps(q_ref[0, :, h * dh:(h + 1) * dh] * scale)
        kc = kc_ref[0, :, h * dh:(h + 1) * dh].astype(BF16)
        vc = vc_ref[0, :, h * dv:(h + 1) * dv].astype(BF16)
        kn = jnp.concatenate([kn_ref[0, :, h * dh:(h + 1) * dh], jnp.zeros((pad, dh), F32)], axis=0).astype(BF16)
        vn = jnp.concatenate([vn_ref[0, :, h * dv:(h + 1) * dv], jnp.zeros((pad, dv), F32)], axis=0).astype(BF16)
        outs = []
        for qm in (q0, q1):
            sc = _dot_nt(qm, kc) + bc_ref[h]
            sn = _dot_nt(qm, kn) + bn_ref[h]
            m = jnp.maximum(jnp.max(sc, axis=-1, keepdims=True), jnp.max(sn, axis=-1, keepdims=True))
            pc = jnp.exp(sc - m)
            pn = jnp.exp(sn - m)
            l = jnp.sum(pc, axis=-1, keepdims=True) + jnp.sum(pn, axis=-1, keepdims=True)
            outs.append((_dot(pc.astype(BF16), vc) + _dot(pn.astype(BF16), vn)) / l)
        o_ref[0, :, h * dv:(h + 1) * dv] = _sub_norm(outs[0], outs[1], lam, lam_init, gsub_ref[h])


def _attn_sample_call(qn, kc, vc, kn, vn, bias_c, bias_n, lamv, gsub, lam_init, heads):
    bsz, t, wq = qn.shape
    past = kc.shape[1]
    wv = vc.shape[2]
    dh = wq // heads
    scale = float((dh // 2) ** -0.5)
    row = lambda shape: pl.BlockSpec((1,) + shape[1:], lambda b: (b, 0, 0))
    full = lambda shape: pl.BlockSpec(shape, lambda b: (0,) * len(shape))
    return pl.pallas_call(
        functools.partial(_attn_sample_kernel, lam_init, scale, heads),
        out_shape=jax.ShapeDtypeStruct((bsz, t, wv), F32),
        grid=(bsz,),
        in_specs=[row(qn.shape), row(kc.shape), row(vc.shape), row(kn.shape), row(vn.shape),
                  full(bias_c.shape), full(bias_n.shape), full(lamv.shape), full(gsub.shape)],
        out_specs=row((bsz, t, wv)),
        compiler_params=_cparams(("arbitrary",)),
        name="attn_sample",
    )(qn, kc, vc, kn, vn, bias_c, bias_n, lamv, gsub)


def _k3_kernel(x_ref, ya_ref, yb_ref, o_ref, gt_ref, sh_ref, sc_ref, g_ref, wout_ref, wpq_ref, keys_ref,
               x1_ref, st_ref, h2t_ref):
    wa = ya_ref.shape[2]
    wb = yb_ref.shape[2]
    y = (_dot(ya_ref[0].astype(BF16), wout_ref[0:wa, :])
         + _dot(yb_ref[0].astype(BF16), wout_ref[wa:wa + wb, :])
         + _dot(o_ref[0].astype(BF16), wout_ref[wa + wb:, :]))
    x1 = x_ref[0] + gt_ref[0] * y
    x1_ref[0] = x1
    h2 = _rms_rows(x1) * g_ref[...] * (1.0 + sc_ref[0]) + sh_ref[0]
    h2t_ref[...] = h2.T.astype(BF16)
    qp = _dot(h2.astype(BF16), wpq_ref[...])
    nk = keys_ref.shape[1]
    kd = keys_ref.shape[2]
    for r in range(keys_ref.shape[0]):
        st_ref[r * nk:(r + 1) * nk, :] = _dot_nt(keys_ref[r], qp[:, r * kd:(r + 1) * kd].astype(BF16))


def _k3_call(x, ya, yb, o, gt, sh, sc, g, w_out, w_pq, keys, tm, per_token):
    bsz, seq, d = x.shape
    n = bsz * seq
    ns = seq // tm
    nrow = keys.shape[0] * keys.shape[1]
    flat = lambda b, s: (0, b * ns + s)
    return pl.pallas_call(
        _k3_kernel,
        out_shape=[jax.ShapeDtypeStruct((bsz, seq, d), F32),
                   jax.ShapeDtypeStruct((nrow, n), F32),
                   jax.ShapeDtypeStruct((d, n), BF16)],
        grid=(bsz, ns),
        in_specs=[_tok_spec(tm, d), _tok_spec(tm, ya.shape[2]), _tok_spec(tm, yb.shape[2]),
                  _tok_spec(tm, o.shape[2]),
                  _mod_spec(per_token, tm, d), _mod_spec(per_token, tm, d), _mod_spec(per_token, tm, d),
                  _full_spec(g.shape), _full_spec(w_out.shape), _full_spec(w_pq.shape),
                  _full_spec(keys.shape)],
        out_specs=[_tok_spec(tm, d), pl.BlockSpec((nrow, tm), flat), pl.BlockSpec((d, tm), flat)],
        compiler_params=_cparams(("arbitrary", "arbitrary")),
        name="out_proj_peer_scores",
    )(x, ya, yb, o, gt, sh, sc, g, w_out, w_pq, keys)


def _sort_network(n):
    def merge(lo, hi, r):
        step = r * 2
        if step < hi - lo:
            yield from merge(lo, hi, step)
            yield from merge(lo + r, hi, step)
            yield from ((i, i + r) for i in range(lo + r, hi - r, step))
        else:
            yield (lo, lo + r)

    def sort(lo, hi):
        if hi - lo >= 1:
            mid = lo + (hi - lo) // 2
            yield from sort(lo, mid)
            yield from sort(mid + 1, hi)
            yield from merge(lo, hi, 1)

    return list(sort(0, n - 1))


def _compare_exchange(vs, i, j):
    a, b = vs[i], vs[j]
    if b is None:
        return
    if a is None:
        vs[i], vs[j] = b, None
        return
    vs[i], vs[j] = jnp.maximum(a, b), jnp.minimum(a, b)


def _sort_desc(vs):
    vs = list(vs)
    for i, j in _sort_network(len(vs)):
        _compare_exchange(vs, i, j)
    return vs


def _top_across_sublanes(vs):
    n = len(vs)
    for shift in (4, 2, 1):
        mixed = []
        for k in range(n):
            a, b = vs[k], vs[n - 1 - k]
            if b is None:
                mixed.append(a)
            else:
                b = pltpu.roll(b, shift, axis=0)
                mixed.append(b if a is None else jnp.maximum(a, b))
        vs = mixed
        d = n // 2
        while d >= 1:
            for k in range(n):
                if k & d == 0:
                    _compare_exchange(vs, k, k + d)
            d //= 2
    return vs


def _all_sublanes(x, op):
    for shift in (4, 2, 1):
        x = op(x, pltpu.roll(x, shift, axis=0))
    return x


def _next_below(vals, cut, k):
    cnt = None
    below = None
    for v in vals:
        c = jnp.where(v >= cut, 1.0, 0.0)
        b = jnp.where(v < cut, v, -jnp.inf)
        cnt = c if cnt is None else cnt + c
        below = b if below is None else jnp.maximum(below, b)
    cnt = _all_sublanes(cnt, jnp.add)
    below = _all_sublanes(below, jnp.maximum)
    return jnp.where(cnt > float(k), cut, below)


PEER_TOP_ROWS = 24


def _k4_kernel(st_ref, stats_ref, top_ref):
    nlist = top_ref.shape[0]
    nk = st_ref.shape[0] // nlist
    k = PEER_TOPK
    assert nk // SUBLANES == k and k == 2 * SUBLANES

    def chunk(c, _):
        cs = pl.ds(pl.multiple_of(c * LANES, LANES), LANES)

        def lists(r, _):
            base = pl.multiple_of(r * nk, nk)
            vals = [st_ref[pl.ds(base + g * SUBLANES, SUBLANES), cs] for g in range(nk // SUBLANES)]
            top = _top_across_sublanes(_sort_desc(vals))
            for i in range(k):
                top_ref[r, i:i + 1, :] = top[i][0:1]
            top_ref[r, k:k + 1, :] = _next_below(vals, top[k - 1], k)[0:1]
            return 0

        lax.fori_loop(0, nlist, lists, 0)

        def heads(h, _):
            a_hi = top_ref[2 * h, SUBLANES:k]
            b_lo, b_hi = top_ref[2 * h + 1, 0:SUBLANES], top_ref[2 * h + 1, SUBLANES:k]
            a0, b0 = top_ref[2 * h, 0:1], top_ref[2 * h + 1, 0:1]
            cands = [a0 + b_lo, a0 + b_hi]
            cands += [top_ref[2 * h, p:p + 1] + b_lo for p in range(1, SUBLANES)]
            cands.append(a_hi + b0)
            tv = _top_across_sublanes(_sort_desc(cands + [None] * (k - len(cands))))
            v16 = tv[k - 1]
            v17 = _next_below(cands, v16, k)
            v17 = jnp.maximum(v17, jnp.maximum(top_ref[2 * h, k:k + 1] + b0, a0 + top_ref[2 * h + 1, k:k + 1]))
            z = jnp.ones_like(v16)
            for r in range(1, k):
                z = z + jnp.exp(tv[r] - tv[0])
            zero = jnp.zeros_like(a0)
            stats_ref[h, :, cs] = jnp.concatenate(
                [(0.5 * (v16 + v17))[0:1], a0, b0, (1.0 / z)[0:1], zero, zero, zero, zero], axis=0)
            return 0

        lax.fori_loop(0, nlist // 2, heads, 0)
        return 0

    lax.fori_loop(0, st_ref.shape[1] // LANES, chunk, 0)


def _k4_call(st, nlist, tt):
    nrow, n = st.shape
    nh = nlist // 2
    return pl.pallas_call(
        _k4_kernel,
        out_shape=jax.ShapeDtypeStruct((nh, SUBLANES, n), F32),
        grid=(n // tt,),
        in_specs=[pl.BlockSpec((nrow, tt), lambda i: (0, i))],
        out_specs=pl.BlockSpec((nh, SUBLANES, tt), lambda i: (0, 0, i)),
        scratch_shapes=[pltpu.VMEM((nlist, PEER_TOP_ROWS, LANES), F32)],
        compiler_params=_cparams(("arbitrary",)),
        name="peer_topk_stats",
    )(st)


PEER_ROWS = 32


PEER_PARTS = 4


def _k5_kernel(nheads, nk, ne, ht_ref, st_ref, stats_ref, u_ref, vt_ref, out_ref,
               c1_ref, e1_ref, e2_ref, at0_ref, at1_ref, g0_ref, g1_ref, acc_ref):
    s = pl.program_id(0)
    sb = jnp.maximum(s - 1, 0)
    sc = jnp.maximum(s - 2, 0)
    e = sb % ne
    eb = u_ref.shape[0]
    tt = ht_ref.shape[1]
    d = vt_ref.shape[0]
    ni = eb // nk
    assert ni == SUBLANES and ni % PEER_PARTS == 0, (eb, nk)
    grp = nk // SUBLANES
    sqrt_half = math.sqrt(0.5)

    @pl.when(s == 0)
    def _():
        for r in (at0_ref, at1_ref, g0_ref, g1_ref):
            r[...] = jnp.zeros_like(r)

    @pl.when(sc % ne == 0)
    def _():
        acc_ref[...] = jnp.zeros_like(acc_ref)

    @pl.when(e == 0)
    def _():
        for h in range(nheads):
            m1 = stats_ref[h, 1:2, :]
            thr = stats_ref[h, 0:1, :]
            for g in range(grp):
                s1 = st_ref[2 * h * nk + g * SUBLANES:2 * h * nk + (g + 1) * SUBLANES, :]
                c1_ref[h * grp + g] = thr - s1
                e1_ref[h * grp + g] = jnp.exp(s1 - m1)
            s2 = st_ref[(2 * h + 1) * nk:(2 * h + 2) * nk, :]
            e2_ref[h * nk:(h + 1) * nk, :] = jnp.exp(s2 - stats_ref[h, 2:3, :]) * stats_ref[h, 3:4, :]

    assert tt // LANES == PEER_PARTS
    mh_rows = eb // 2
    dh_rows = d // 2
    nh_cols = tt // 2

    def part(at_w, at_r, g_w, g_r, k, _):
        mh = k // 2
        nc = pl.ds(pl.multiple_of((k % 2) * nh_cols, nh_cols), nh_cols)
        er = pl.ds(pl.multiple_of(mh * mh_rows, mh_rows), mh_rows)
        at_w[er, nc] = _dot(u_ref[er, :], ht_ref[:, nc])
        cs = pl.ds(pl.multiple_of(k * LANES, LANES), LANES)
        for j0 in range(0, nk, PEER_ROWS):
            w = [jnp.zeros((PEER_ROWS, LANES), F32) for _ in range(ni)]
            for h in range(nheads):
                s2 = st_ref[(2 * h + 1) * nk + j0:(2 * h + 1) * nk + j0 + PEER_ROWS, cs]
                e2 = e2_ref[h * nk + j0:h * nk + j0 + PEER_ROWS, cs]
                for ii in range(ni):
                    sel = s2 >= c1_ref[h * grp + e, ii:ii + 1, cs]
                    w[ii] = w[ii] + e1_ref[h * grp + e, ii:ii + 1, cs] * jnp.where(sel, e2, 0.0)
            for ii in range(ni):
                rows = slice(ii * nk + j0, ii * nk + j0 + PEER_ROWS)
                a = at_r[rows, cs]
                gelu = 0.5 * a * (1.0 + lax.erf(a * sqrt_half))
                g_w[rows, cs] = (w[ii] * gelu).astype(BF16)
        dr = pl.ds(pl.multiple_of(mh * dh_rows, dh_rows), dh_rows)
        acc_ref[dr, nc] += _dot(vt_ref[dr, :], g_r[:, nc])
        return 0

    @pl.when(s % 2 == 0)
    def _():
        lax.fori_loop(0, PEER_PARTS, functools.partial(part, at0_ref, at1_ref, g1_ref, g0_ref), 0)

    @pl.when(s % 2 == 1)
    def _():
        lax.fori_loop(0, PEER_PARTS, functools.partial(part, at1_ref, at0_ref, g0_ref, g1_ref), 0)

    @pl.when(jnp.logical_and(s >= 2, sc % ne == ne - 1))
    def _():
        out_ref[...] = acc_ref[...].T


def _k5_call(h2t, st, stats, u_bf, vt_bf, nheads, nk, tt, eb):
    d, n = h2t.shape
    ne = u_bf.shape[0] // eb
    nt = n // tt
    tile_a = lambda s: jnp.minimum(s // ne, nt - 1)
    tile_b = lambda s: jnp.minimum(jnp.maximum(s - 1, 0) // ne, nt - 1)
    tile_c = lambda s: jnp.minimum(jnp.maximum(s - 2, 0) // ne, nt - 1)
    return pl.pallas_call(
        functools.partial(_k5_kernel, nheads, nk, ne),
        out_shape=jax.ShapeDtypeStruct((n, d), F32),
        grid=(nt * ne + 2,),
        in_specs=[pl.BlockSpec((d, tt), lambda s: (0, tile_a(s))),
                  pl.BlockSpec((st.shape[0], tt), lambda s: (0, tile_b(s))),
                  pl.BlockSpec((nheads, SUBLANES, tt), lambda s: (0, 0, tile_b(s))),
                  pl.BlockSpec((eb, d), lambda s: (s % ne, 0)),
                  pl.BlockSpec((d, eb), lambda s: (0, jnp.maximum(s - 2, 0) % ne))],
        out_specs=pl.BlockSpec((tt, d), lambda s: (tile_c(s), 0)),
        scratch_shapes=[pltpu.VMEM((nheads * nk // SUBLANES, SUBLANES, tt), F32),
                        pltpu.VMEM((nheads * nk // SUBLANES, SUBLANES, tt), F32),
                        pltpu.VMEM((nheads * nk, tt), F32),
                        pltpu.VMEM((eb, tt), F32), pltpu.VMEM((eb, tt), F32),
                        pltpu.VMEM((eb, tt), BF16), pltpu.VMEM((eb, tt), BF16),
                        pltpu.VMEM((d, tt), F32)],
        compiler_params=_cparams(("arbitrary",)),
        name="peer_dense",
    )(h2t, st, stats, u_bf, vt_bf)


def _res_kernel(x_ref, po_ref, gt_ref, o_ref):
    o_ref[0] = x_ref[0] + gt_ref[0] * po_ref[0]


def _res_call(x, po, gt, tm, per_token):
    bsz, seq, d = x.shape
    return pl.pallas_call(
        _res_kernel,
        out_shape=jax.ShapeDtypeStruct((bsz, seq, d), F32),
        grid=(bsz, seq // tm),
        in_specs=[_tok_spec(tm, d), _tok_spec(tm, d), _mod_spec(per_token, tm, d)],
        out_specs=_tok_spec(tm, d),
        compiler_params=_cparams(("arbitrary", "arbitrary")),
        name="final_residual",
    )(x, po, gt)


def _t5_bucket(rel):
    nb = REL_BUCKETS // 2
    max_exact = nb // 2
    side = jnp.where(rel > 0, nb, 0)
    n = jnp.abs(rel)
    nf = jnp.maximum(n, 1).astype(jnp.float32)
    large = max_exact + (jnp.log(nf / max_exact) / math.log(REL_MAX_DIST / max_exact)
                         * (nb - max_exact)).astype(jnp.int32)
    large = jnp.minimum(large, nb - 1)
    return side + jnp.where(n < max_exact, n, large)


def _bias_table(rel_bias, q_pos, k_pos):
    b = rel_bias.astype(F32)[_t5_bucket(k_pos[None, :] - q_pos[:, None])]
    return jnp.transpose(b, (2, 0, 1))


def _block_diag(w):
    g, a, b = w.shape
    eye = jnp.eye(g, dtype=w.dtype)
    return (eye[:, None, :, None] * w[:, :, None, :]).reshape(g * a, g * b)


def _trunk(x, c_mod, per_token, sample, layer_w, rel_bias, tm, tq, tt, eb):
    bsz, seq, d = x.shape
    depth = len(layer_w)
    k_out, v_out, pool_out, gv_out = [], [], [], []
    tok = (lambda t: t.reshape(1, bsz * seq, t.shape[-1])) if per_token else (lambda t: t)
    seqv = lambda t: t.reshape(bsz, seq, t.shape[-1])
    xt = tok(x)
    res = None
    for l in range(depth):
        w = layer_w[l]
        sh1, sc1, gt1, sh2, sc2, gt2 = c_mod[l]
        heads = w["heads"]
        widths = w["widths"]
        outs = _k1_call(xt, res, sh1, sc1, w["g1"], w["w_in"], w["p256"], w["g_gv"],
                        w["gq"], w["gk"], widths, tm, per_token)
        if res is not None:
            xt, outs = outs[0], outs[1:]
        za, zu, vn, qn, kn, v = [seqv(t) for t in outs]
        if sample is None:
            hist = jnp.zeros((bsz, POOL_CARRY, za.shape[2]), F32)
            ya, yb = _k2_call(za, hist, zu, vn, w["wp_bd"], w["s_pool"], w["w_s"], w["bsp"], 0, 2 * GMLP_CHUNK)
            pool_out.append(za[:, seq - (POOL_CARRY - 1):])
        else:
            state_pool, cache_k, cache_v = sample
            hist = jnp.pad(state_pool[l], ((0, 0), (1, 0), (0, 0)))
            padr = ((0, 0), (0, GMLP_CHUNK - seq), (0, 0))
            ya, yb = _k2_call(jnp.pad(za, padr), hist, jnp.pad(zu, padr), jnp.pad(vn, padr), w["wp_bd"],
                              w["s_pool"], w["w_s"], w["bsp"], cache_k.shape[2], GMLP_CHUNK)
            ya, yb = ya[:, :seq], yb[:, :seq]
            pool_out.append(jnp.concatenate([state_pool[l], za], axis=1)[:, -(POOL_CARRY - 1):])
            gv_out.append(vn)
        lam_init = 0.8 - 0.6 * math.exp(-0.3 * l)
        if sample is None:
            pos = jnp.arange(tq)
            tiles = jnp.stack([_bias_table(rel_bias, pos, pos - dlt * tq) for dlt in range(3)], axis=1)
            o = _attn_prompt_call(qn, kn, v, tiles, w["lamv"], w["g_sub"], lam_init, heads, tq)
        else:
            past = cache_k.shape[2]
            q_pos = past + jnp.arange(seq)
            bias_c = _bias_table(rel_bias, q_pos, jnp.arange(past))
            bias_n = _bias_table(rel_bias, q_pos, q_pos)
            bias_n = jnp.pad(bias_n, ((0, 0), (0, 0), (0, LANES - seq)), constant_values=NEG_BIG)
            o = _attn_sample_call(qn, cache_k[l].reshape(bsz, past, -1), cache_v[l].reshape(bsz, past, -1),
                                  kn, v, bias_c, bias_n, w["lamv"], w["g_sub"], lam_init, heads)
        k_out.append(kn.reshape(bsz, seq, heads, -1))
        v_out.append(v.reshape(bsz, seq, heads, -1))
        xt, st, h2t = _k3_call(xt, tok(ya), tok(yb), tok(o), gt1, sh2, sc2, w["g2"], w["w_out"], w["w_pq"],
                               w["keys"], tm, per_token)
        nlist = w["keys"].shape[0]
        nk = w["keys"].shape[1]
        stats = _k4_call(st, nlist, tt)
        po = _k5_call(h2t, st, stats, w["u"], w["vt"], nlist // 2, nk, tt, eb)
        res = (po.reshape(xt.shape), gt2)
    y = _res_call(xt, res[0], res[1], tm, per_token).reshape(bsz, seq, d)
    return y, jnp.stack(k_out), jnp.stack(v_out), jnp.stack(pool_out), (jnp.stack(gv_out) if gv_out else None)


def kernel(x_prompt, x_sample, c_prompt, c_sample, cache_k, cache_v, state_pool, rel_bias, w_ada, b_ada,
           g_norm, w_in, w_out, w_pool, s_pool, g_gv, w_s, b_s, g_qk, lam_vecs, g_sub, w_pq, sub_keys,
           u_tab, v_tab):
    depth = w_in.shape[0]
    d = x_prompt.shape[2]
    bp, sp = x_prompt.shape[:2]
    bs, ss = x_sample.shape[:2]
    heads = cache_k.shape[3]
    dqk = cache_k.shape[4]
    dv = cache_v.shape[4]
    wa = w_pool.shape[1] * w_pool.shape[2]
    wg = g_gv.shape[1]
    widths = (wa, wg, heads * dqk, heads * dv)
    grp = dqk // 2

    p256 = _block_diag(jnp.full((256 // grp, grp, grp), 1.0 / grp, F32)).astype(BF16)
    c_all = jnp.concatenate([c_prompt, c_sample], axis=0)
    layer_w, mods_p, mods_s = [], [], []
    for l in range(depth):
        mod = _ada_call(c_all, w_ada[l], b_ada[l])
        six = jnp.split(mod, 6, axis=-1)
        mods_p.append([m[:bp, None, :] for m in six])
        mods_s.append([jnp.broadcast_to(m[bp:, None, :], (bs, ss, d)).reshape(1, bs * ss, d) for m in six])
        nh, two, nk, kd = sub_keys.shape[1:]
        layer_w.append(dict(
            heads=heads, widths=widths,
            g1=g_norm[l, 0].reshape(1, d), g2=g_norm[l, 1].reshape(1, d),
            w_in=w_in[l].astype(BF16), w_out=w_out[l].astype(BF16), w_pq=w_pq[l].astype(BF16),
            p256=p256, g_gv=g_gv[l].reshape(1, wg),
            gq=jnp.tile(g_qk[l, 0], heads * dqk // grp).reshape(1, heads * dqk),
            gk=jnp.tile(g_qk[l, 1], heads * dqk // grp).reshape(1, heads * dqk),
            wp_bd=_block_diag(w_pool[l]).astype(BF16), s_pool=s_pool[l].reshape(1, wa),
            w_s=w_s[l], bsp=jnp.repeat(b_s[l].T, wg // b_s.shape[1], axis=1),
            lamv=lam_vecs[l], g_sub=g_sub[l].reshape(heads, 1, dv),
            keys=sub_keys[l].reshape(nh * two, nk, kd).astype(BF16),
            u=u_tab[l].astype(BF16), vt=v_tab[l].T.astype(BF16)))

    tm = min(256, sp)
    y_p, k_p, v_p, pool_p, _ = _trunk(x_prompt, mods_p, False, None, layer_w, rel_bias,
                                      tm=tm, tq=min(256, sp), tt=min(512, bp * sp), eb=SUBLANES * sub_keys.shape[3])
    y_s, k_s, v_s, pool_s, gv_s = _trunk(x_sample, mods_s, True, (state_pool, cache_k, cache_v), layer_w,
                                         rel_bias, tm=min(256, bs * ss), tq=None, tt=min(512, bs * ss), eb=SUBLANES * sub_keys.shape[3])
    return (y_p, y_s, k_p, v_p, pool_p, k_s, v_s, pool_s, gv_s)
```

```python
import functools
import math

import jax
import jax.numpy as jnp
from jax import lax
from jax.experimental import pallas as pl
from jax.experimental.pallas import tpu as pltpu

F32 = jnp.float32
BF16 = jnp.bfloat16

EPS = 1e-6
ATT_CHUNK = 64
POOL_WINDOWS = (2, 4, 8, 16)
POOL_CARRY = 16
GMLP_CHUNK = 128
REL_BUCKETS = 32
REL_MAX_DIST = 128
PEER_TOPK = 16
NEG_BIG = -1e30

LANES = 128
SUBLANES = 8
VMEM_LIMIT = 56 * 1024 * 1024


def _cparams(sem, flags=None):
    return pltpu.CompilerParams(dimension_semantics=sem, vmem_limit_bytes=VMEM_LIMIT, flags=flags)


def _dot(a, b):
    return lax.dot_general(a, b, (((1,), (0,)), ((), ())), preferred_element_type=F32)


def _idiv_pow2(x, n):
    assert n & (n - 1) == 0, n
    return jnp.right_shift(x, n.bit_length() - 1)


def _dot_nt(a, b):
    return lax.dot_general(a, b, (((1,), (1,)), ((), ())), preferred_element_type=F32)


def _ada_kernel(c_ref, w_ref, b_ref, o_ref):
    c = c_ref[...]
    s = c * jax.nn.sigmoid(c)
    o_ref[...] = _dot(s, w_ref[...]) + b_ref[...]


def _ada_call(c, w, b):
    m, d = c.shape
    n = w.shape[1]
    nb = 1536
    return pl.pallas_call(
        _ada_kernel,
        out_shape=jax.ShapeDtypeStruct((m, n), F32),
        grid=(n // nb,),
        in_specs=[pl.BlockSpec((m, d), lambda j: (0, 0)),
                  pl.BlockSpec((d, nb), lambda j: (0, j)),
                  pl.BlockSpec((1, nb), lambda j: (0, j))],
        out_specs=pl.BlockSpec((m, nb), lambda j: (0, j)),
        compiler_params=_cparams(("arbitrary",)),
        name="ada_mod",
    )(c, w, b.reshape(1, n))


def _rms_rows(x):
    return x * lax.rsqrt(jnp.mean(x * x, axis=-1, keepdims=True) + EPS)


def _group_norm_256(t, p_ref):
    pieces = []
    for s in range(t.shape[1] // 256):
        ts = t[:, s * 256:(s + 1) * 256]
        msq = _dot((ts * ts).astype(BF16), p_ref[...])
        pieces.append(ts * lax.rsqrt(msq + EPS))
    return pieces[0] if len(pieces) == 1 else jnp.concatenate(pieces, axis=-1)


def _k1_kernel(has_res, widths, *refs):
    if has_res:
        x_ref, po_ref, gt_ref = refs[:3]
        refs = refs[3:]
    else:
        x_ref = refs[0]
        refs = refs[1:]
    sh_ref, sc_ref, g_ref, win_ref, p_ref, gv_ref, gq_ref, gk_ref = refs[:8]
    outs = refs[8:]
    x = x_ref[0]
    if has_res:
        x = x + gt_ref[0] * po_ref[0]
        outs[0][0] = x
        outs = outs[1:]
    za_ref, zu_ref, vn_ref, q_ref, k_ref, v_ref = outs
    h = _rms_rows(x) * g_ref[...] * (1.0 + sc_ref[0]) + sh_ref[0]
    z = _dot(h.astype(BF16), win_ref[...])
    wa, wg, wqk, wv = widths
    o = 0
    za_ref[0] = z[:, o:o + wa]; o += wa
    zu_ref[0] = z[:, o:o + wg]; o += wg
    vn_ref[0] = _group_norm_256(z[:, o:o + wg], p_ref) * gv_ref[...]; o += wg
    q_ref[0] = _group_norm_256(z[:, o:o + wqk], p_ref) * gq_ref[...]; o += wqk
    k_ref[0] = _group_norm_256(z[:, o:o + wqk], p_ref) * gk_ref[...]; o += wqk
    v_ref[0] = z[:, o:o + wv]


def _tok_spec(tm, d):
    return pl.BlockSpec((1, tm, d), lambda b, s: (b, s, 0))


def _mod_spec(per_token, tm, d):
    if per_token:
        return pl.BlockSpec((1, tm, d), lambda b, s: (b, s, 0))
    return pl.BlockSpec((1, 1, d), lambda b, s: (b, 0, 0))


def _full_spec(shape):
    nd = len(shape)
    return pl.BlockSpec(shape, lambda b, s: (0,) * nd)


def _k1_call(x, res, sh, sc, g, w_in, p256, gv, gq, gk, widths, tm, per_token):
    bsz, seq, d = x.shape
    wa, wg, wqk, wv = widths
    has_res = res is not None
    ins, specs = [x], [_tok_spec(tm, d)]
    if has_res:
        po, gt = res
        ins += [po, gt]
        specs += [_tok_spec(tm, d), _mod_spec(per_token, tm, d)]
    ins += [sh, sc, g, w_in, p256, gv, gq, gk]
    specs += [_mod_spec(per_token, tm, d), _mod_spec(per_token, tm, d), _full_spec(g.shape),
              _full_spec(w_in.shape), _full_spec(p256.shape), _full_spec(gv.shape),
              _full_spec(gq.shape), _full_spec(gk.shape)]
    out_w = ([d] if has_res else []) + [wa, wg, wg, wqk, wqk, wv]
    return pl.pallas_call(
        functools.partial(_k1_kernel, has_res, widths),
        out_shape=[jax.ShapeDtypeStruct((bsz, seq, w), F32) for w in out_w],
        grid=(bsz, seq // tm),
        in_specs=specs,
        out_specs=[_tok_spec(tm, w) for w in out_w],
        compiler_params=_cparams(("arbitrary", "arbitrary")),
        name="in_proj",
    )(*ins)


def _k2_kernel(pos0, za_ref, hist_ref, zu_ref, vn_ref, wp_ref, sp_ref, ws_ref, bsp_ref,
               ya_ref, yb_ref, carry_ref):
    s = pl.program_id(1)
    tb = za_ref.shape[1]
    w = za_ref.shape[2]
    gw = w // len(POOL_WINDOWS)

    @pl.when(s == 0)
    def _():
        carry_ref[...] = hist_ref[0]

    za = za_ref[0]
    ext = jnp.concatenate([carry_ref[...], za], axis=0)
    carry_ref[...] = za[tb - POOL_CARRY:, :]
    sums = {1: ext}
    d = 1
    while d < max(POOL_WINDOWS):
        sums[2 * d] = sums[d] + pltpu.roll(sums[d], d, axis=0)
        d *= 2
    row = lax.broadcasted_iota(jnp.int32, (tb, w), 0)
    lane_grp = _idiv_pow2(lax.broadcasted_iota(jnp.int32, (tb, w), 1), gw)
    pos1 = (pos0 + s * tb + row + 1).astype(F32)
    means = None
    for gi, win in enumerate(POOL_WINDOWS):
        m = sums[win][POOL_CARRY:, :] / jnp.minimum(float(win), pos1)
        means = m if means is None else jnp.where(lane_grp == gi, m, means)
    dlt = means - za
    ya_ref[0] = _dot(dlt.astype(BF16), wp_ref[...]) * sp_ref[...]

    ll = ws_ref.shape[1]
    r_i = lax.broadcasted_iota(jnp.int32, (ll, ll), 0)
    c_i = lax.broadcasted_iota(jnp.int32, (ll, ll), 1)
    grp = _idiv_pow2(lax.broadcasted_iota(jnp.int32, (ll, w), 1), w // ws_ref.shape[0])
    for c in range(tb // ll):
        vnc = vn_ref[0, c * ll:(c + 1) * ll, :].astype(BF16)
        spv = None
        for gi in range(ws_ref.shape[0]):
            wsg = jnp.where(r_i >= c_i, ws_ref[gi], 0.0).astype(BF16)
            t = _dot(wsg, vnc)
            spv = t if spv is None else jnp.where(grp == gi, t, spv)
        yb_ref[0, c * ll:(c + 1) * ll, :] = zu_ref[0, c * ll:(c + 1) * ll, :] * (spv + bsp_ref[...])


def _k2_call(za, hist16, zu, vn, wp_bd, s_pool, ws, bsp, pos0, tb):
    bsz, seq, w = za.shape
    return pl.pallas_call(
        functools.partial(_k2_kernel, pos0),
        out_shape=[jax.ShapeDtypeStruct((bsz, seq, w), F32)] * 2,
        grid=(bsz, seq // tb),
        in_specs=[_tok_spec(tb, w),
                  pl.BlockSpec((1, POOL_CARRY, w), lambda b, s: (b, 0, 0)),
                  _tok_spec(tb, w), _tok_spec(tb, w),
                  _full_spec(wp_bd.shape), _full_spec(s_pool.shape), _full_spec(ws.shape),
                  _full_spec(bsp.shape)],
        out_specs=[_tok_spec(tb, w)] * 2,
        scratch_shapes=[pltpu.VMEM((POOL_CARRY, w), F32)],
        compiler_params=_cparams(("arbitrary", "arbitrary")),
        name="pool_gmlp",
    )(za, hist16, zu, vn, wp_bd, s_pool, ws, bsp)


def _lambda(lamv_ref, lam_init):
    lv = lamv_ref[...]
    a = jnp.sum(lv[0:1] * lv[1:2], axis=-1, keepdims=True)
    b = jnp.sum(lv[2:3] * lv[3:4], axis=-1, keepdims=True)
    return jnp.exp(a) - jnp.exp(b) + lam_init


def _split_maps(q):
    lane = lax.broadcasted_iota(jnp.int32, q.shape, 1)
    half = q.shape[1] // 2
    return (jnp.where(lane < half, q, 0.0).astype(BF16),
            jnp.where(lane >= half, q, 0.0).astype(BF16))


def _online_update(state, s, vblk):
    m, l, acc = state
    m_new = jnp.maximum(m, jnp.max(s, axis=-1, keepdims=True))
    alpha = jnp.exp(m - m_new)
    p = jnp.exp(s - m_new)
    l_new = alpha * l + jnp.sum(p, axis=-1, keepdims=True)
    acc_new = alpha * acc + _dot(p.astype(BF16), vblk)
    return m_new, l_new, acc_new


def _sub_norm(o0, o1, lam, lam_init, gsub):
    o = o0 - lam * o1
    return _rms_rows(o) * gsub * (1.0 - lam_init)


def _attn_prompt_kernel(lam_init, scale, q_ref, k_ref, v_ref, bias_ref, lamv_ref, gsub_ref, o_ref):
    qb = pl.program_id(2)
    tq = q_ref.shape[1]
    dv = v_ref.shape[2]
    q0, q1 = _split_maps(q_ref[0] * scale)

    def init():
        return (jnp.full((tq, 1), NEG_BIG, F32), jnp.zeros((tq, 1), F32), jnp.zeros((tq, dv), F32))

    def block(kb, bias):
        kblk = k_ref[0, pl.ds(pl.multiple_of(kb * tq, tq), tq), :].astype(BF16)
        vblk = v_ref[0, pl.ds(pl.multiple_of(kb * tq, tq), tq), :].astype(BF16)
        return _dot_nt(q0, kblk) + bias, _dot_nt(q1, kblk) + bias, vblk

    def body(kb, carry):
        st0, st1 = carry
        bias = bias_ref[0, jnp.minimum(qb - kb, 2)]
        s0, s1, vblk = block(kb, bias)
        return _online_update(st0, s0, vblk), _online_update(st1, s1, vblk)

    st0, st1 = lax.fori_loop(0, qb, body, (init(), init()))
    s0, s1, vblk = block(qb, bias_ref[0, 0])
    qi = _idiv_pow2(lax.broadcasted_iota(jnp.int32, (tq, tq), 0), ATT_CHUNK)
    ki = _idiv_pow2(lax.broadcasted_iota(jnp.int32, (tq, tq), 1), ATT_CHUNK)
    vis = ki <= qi
    st0 = _online_update(st0, jnp.where(vis, s0, NEG_BIG), vblk)
    st1 = _online_update(st1, jnp.where(vis, s1, NEG_BIG), vblk)
    lam = _lambda(lamv_ref, lam_init)
    o_ref[0] = _sub_norm(st0[2] / st0[1], st1[2] / st1[1], lam, lam_init, gsub_ref[0])


def _attn_prompt_call(qn, kn, v, bias_tiles, lamv, gsub, lam_init, heads, tq):
    bsz, seq, _ = qn.shape
    dh = qn.shape[2] // heads
    dv = v.shape[2] // heads
    scale = float((dh // 2) ** -0.5)
    return pl.pallas_call(
        functools.partial(_attn_prompt_kernel, lam_init, scale),
        out_shape=jax.ShapeDtypeStruct((bsz, seq, heads * dv), F32),
        grid=(bsz, heads, seq // tq),
        in_specs=[pl.BlockSpec((1, tq, dh), lambda b, h, i: (b, i, h)),
                  pl.BlockSpec((1, seq, dh), lambda b, h, i: (b, 0, h)),
                  pl.BlockSpec((1, seq, dv), lambda b, h, i: (b, 0, h)),
                  pl.BlockSpec((1, 3, tq, tq), lambda b, h, i: (h, 0, 0, 0)),
                  pl.BlockSpec(lamv.shape, lambda b, h, i: (0, 0)),
                  pl.BlockSpec((1, 1, dv), lambda b, h, i: (h, 0, 0))],
        out_specs=pl.BlockSpec((1, tq, dv), lambda b, h, i: (b, i, h)),
        compiler_params=_cparams(("arbitrary", "arbitrary", "arbitrary")),
        name="attn_prompt",
    )(qn, kn, v, bias_tiles, lamv, gsub)


def _attn_sample_kernel(lam_init, scale, heads, q_ref, kc_ref, vc_ref, kn_ref, vn_ref,
                        bc_ref, bn_ref, lamv_ref, gsub_ref, o_ref):
    t = q_ref.shape[1]
    dh = q_ref.shape[2] // heads
    dv = vc_ref.shape[3]
    lam = _lambda(lamv_ref, lam_init)
    pad = bn_ref.shape[2] - t
    for h in range(heads):
        q0, q1 = _split_maps(q_ref[0, :, h * dh:(h + 1) * dh] * scale)
        kc = kc_ref[0, :, h, :].astype(BF16)
        vc = vc_ref[0, :, h, :].astype(BF16)
        kn = jnp.concatenate([kn_ref[0, :, h * dh:(h + 1) * dh], jnp.zeros((pad, dh), F32)], axis=0).astype(BF16)
        vn = jnp.concatenate([vn_ref[0, :, h * dv:(h + 1) * dv], jnp.zeros((pad, dv), F32)], axis=0).astype(BF16)
        outs = []
        for qm in (q0, q1):
            sc = _dot_nt(qm, kc) + bc_ref[h]
            sn = _dot_nt(qm, kn) + bn_ref[h]
            m = jnp.maximum(jnp.max(sc, axis=-1, keepdims=True), jnp.max(sn, axis=-1, keepdims=True))
            pc = jnp.exp(sc - m)
            pn = jnp.exp(sn - m)
            l = jnp.sum(pc, axis=-1, keepdims=True) + jnp.sum(pn, axis=-1, keepdims=True)
            outs.append((_dot(pc.astype(BF16), vc) + _dot(pn.astype(BF16), vn)) / l)
        o_ref[0, :, h * dv:(h + 1) * dv] = _sub_norm(outs[0], outs[1], lam, lam_init, gsub_ref[h])


def _attn_sample_call(qn, cache_k, cache_v, layer, kn, vn, bias_c, bias_n, lamv, gsub, lam_init, heads):
    bsz, t, wq = qn.shape
    wv = heads * cache_v.shape[4]
    dh = wq // heads
    scale = float((dh // 2) ** -0.5)
    row = lambda shape: pl.BlockSpec((1,) + shape[1:], lambda b: (b, 0, 0))
    full = lambda shape: pl.BlockSpec(shape, lambda b: (0,) * len(shape))
    cache = lambda shape: pl.BlockSpec((None, 1) + shape[2:], lambda b: (layer, b, 0, 0, 0))
    return pl.pallas_call(
        functools.partial(_attn_sample_kernel, lam_init, scale, heads),
        out_shape=jax.ShapeDtypeStruct((bsz, t, wv), F32),
        grid=(bsz,),
        in_specs=[row(qn.shape), cache(cache_k.shape), cache(cache_v.shape), row(kn.shape), row(vn.shape),
                  full(bias_c.shape), full(bias_n.shape), full(lamv.shape), full(gsub.shape)],
        out_specs=row((bsz, t, wv)),
        compiler_params=_cparams(("arbitrary",)),
        name="attn_sample",
    )(qn, cache_k, cache_v, kn, vn, bias_c, bias_n, lamv, gsub)


def _k3_kernel(x_ref, ya_ref, yb_ref, o_ref, gt_ref, sh_ref, sc_ref, g_ref, wout_ref, wpq_ref, keys_ref,
               x1_ref, st_ref, h2t_ref):
    wa = ya_ref.shape[2]
    wb = yb_ref.shape[2]
    y = (_dot(ya_ref[0].astype(BF16), wout_ref[0:wa, :])
         + _dot(yb_ref[0].astype(BF16), wout_ref[wa:wa + wb, :])
         + _dot(o_ref[0].astype(BF16), wout_ref[wa + wb:, :]))
    x1 = x_ref[0] + gt_ref[0] * y
    x1_ref[0] = x1
    h2 = _rms_rows(x1) * g_ref[...] * (1.0 + sc_ref[0]) + sh_ref[0]
    h2t_ref[...] = h2.T.astype(BF16)
    qp = _dot(h2.astype(BF16), wpq_ref[...])
    nk = keys_ref.shape[1]
    kd = keys_ref.shape[2]
    for r in range(keys_ref.shape[0]):
        st_ref[r * nk:(r + 1) * nk, :] = _dot_nt(keys_ref[r], qp[:, r * kd:(r + 1) * kd].astype(BF16))


def _k3_call(x, ya, yb, o, gt, sh, sc, g, w_out, w_pq, keys, tm, per_token):
    bsz, seq, d = x.shape
    n = bsz * seq
    ns = seq // tm
    nrow = keys.shape[0] * keys.shape[1]
    flat = lambda b, s: (0, b * ns + s)
    return pl.pallas_call(
        _k3_kernel,
        out_shape=[jax.ShapeDtypeStruct((bsz, seq, d), F32),
                   jax.ShapeDtypeStruct((nrow, n), F32),
                   jax.ShapeDtypeStruct((d, n), BF16)],
        grid=(bsz, ns),
        in_specs=[_tok_spec(tm, d), _tok_spec(tm, ya.shape[2]), _tok_spec(tm, yb.shape[2]),
                  _tok_spec(tm, o.shape[2]),
                  _mod_spec(per_token, tm, d), _mod_spec(per_token, tm, d), _mod_spec(per_token, tm, d),
                  _full_spec(g.shape), _full_spec(w_out.shape), _full_spec(w_pq.shape),
                  _full_spec(keys.shape)],
        out_specs=[_tok_spec(tm, d), pl.BlockSpec((nrow, tm), flat), pl.BlockSpec((d, tm), flat)],
        compiler_params=_cparams(("arbitrary", "arbitrary")),
        name="out_proj_peer_scores",
    )(x, ya, yb, o, gt, sh, sc, g, w_out, w_pq, keys)


def _sort_network(n):
    def merge(lo, hi, r):
        step = r * 2
        if step < hi - lo:
            yield from merge(lo, hi, step)
            yield from merge(lo + r, hi, step)
            yield from ((i, i + r) for i in range(lo + r, hi - r, step))
        else:
            yield (lo, lo + r)

    def sort(lo, hi):
        if hi - lo >= 1:
            mid = lo + (hi - lo) // 2
            yield from sort(lo, mid)
            yield from sort(mid + 1, hi)
            yield from merge(lo, hi, 1)

    return list(sort(0, n - 1))


def _compare_exchange(vs, i, j):
    a, b = vs[i], vs[j]
    if b is None:
        return
    if a is None:
        vs[i], vs[j] = b, None
        return
    vs[i], vs[j] = jnp.maximum(a, b), jnp.minimum(a, b)


def _sort_desc(vs):
    vs = list(vs)
    for i, j in _sort_network(len(vs)):
        _compare_exchange(vs, i, j)
    return vs


def _top_across_sublanes(vs):
    n = len(vs)
    for shift in (4, 2, 1):
        mixed = []
        for k in range(n):
            a, b = vs[k], vs[n - 1 - k]
            if b is None:
                mixed.append(a)
            else:
                b = pltpu.roll(b, shift, axis=0)
                mixed.append(b if a is None else jnp.maximum(a, b))
        vs = mixed
        d = n // 2
        while d >= 1:
            for k in range(n):
                if k & d == 0:
                    _compare_exchange(vs, k, k + d)
            d //= 2
    return vs


def _all_sublanes(x, op):
    for shift in (4, 2, 1):
        x = op(x, pltpu.roll(x, shift, axis=0))
    return x


def _next_below(vals, cut, k):
    cnt = None
    below = None
    for v in vals:
        c = jnp.where(v >= cut, 1.0, 0.0)
        b = jnp.where(v < cut, v, -jnp.inf)
        cnt = c if cnt is None else cnt + c
        below = b if below is None else jnp.maximum(below, b)
    cnt = _all_sublanes(cnt, jnp.add)
    below = _all_sublanes(below, jnp.maximum)
    return jnp.where(cnt > float(k), cut, below)


PEER_TOP_ROWS = 24


def _k4_kernel(st_ref, stats_ref, top_ref):
    nlist = top_ref.shape[0]
    nk = st_ref.shape[0] // nlist
    k = PEER_TOPK
    assert nk // SUBLANES == k and k == 2 * SUBLANES

    def chunk(c, _):
        cs = pl.ds(pl.multiple_of(c * LANES, LANES), LANES)

        def lists(r, _):
            base = pl.multiple_of(r * nk, nk)
            vals = [st_ref[pl.ds(base + g * SUBLANES, SUBLANES), cs] for g in range(nk // SUBLANES)]
            top = _top_across_sublanes(_sort_desc(vals))
            for i in range(k):
                top_ref[r, i:i + 1, :] = top[i][0:1]
            top_ref[r, k:k + 1, :] = _next_below(vals, top[k - 1], k)[0:1]
            return 0

        lax.fori_loop(0, nlist, lists, 0)

        def heads(h, _):
            a_hi = top_ref[2 * h, SUBLANES:k]
            b_lo, b_hi = top_ref[2 * h + 1, 0:SUBLANES], top_ref[2 * h + 1, SUBLANES:k]
            a0, b0 = top_ref[2 * h, 0:1], top_ref[2 * h + 1, 0:1]
            cands = [a0 + b_lo, a0 + b_hi]
            cands += [top_ref[2 * h, p:p + 1] + b_lo for p in range(1, SUBLANES)]
            cands.append(a_hi + b0)
            tv = _top_across_sublanes(_sort_desc(cands + [None] * (k - len(cands))))
            v16 = tv[k - 1]
            v17 = _next_below(cands, v16, k)
            v17 = jnp.maximum(v17, jnp.maximum(top_ref[2 * h, k:k + 1] + b0, a0 + top_ref[2 * h + 1, k:k + 1]))
            z = jnp.ones_like(v16)
            for r in range(1, k):
                z = z + jnp.exp(tv[r] - tv[0])
            zero = jnp.zeros_like(a0)
            stats_ref[h, :, cs] = jnp.concatenate(
                [(0.5 * (v16 + v17))[0:1], a0, b0, (1.0 / z)[0:1], zero, zero, zero, zero], axis=0)
            return 0

        lax.fori_loop(0, nlist // 2, heads, 0)
        return 0

    lax.fori_loop(0, st_ref.shape[1] // LANES, chunk, 0)


def _k4_call(st, nlist, tt):
    nrow, n = st.shape
    nh = nlist // 2
    return pl.pallas_call(
        _k4_kernel,
        out_shape=jax.ShapeDtypeStruct((nh, SUBLANES, n), F32),
        grid=(n // tt,),
        in_specs=[pl.BlockSpec((nrow, tt), lambda i: (0, i))],
        out_specs=pl.BlockSpec((nh, SUBLANES, tt), lambda i: (0, 0, i)),
        scratch_shapes=[pltpu.VMEM((nlist, PEER_TOP_ROWS, LANES), F32)],
        compiler_params=_cparams(("arbitrary",)),
        name="peer_topk_stats",
    )(st)


PEER_ROWS = 32


PEER_PARTS = 4


def _k5_kernel(nheads, nk, ne, ht_ref, st_ref, stats_ref, u_ref, vt_ref, out_ref,
               c1_ref, e1_ref, e2_ref, at0_ref, at1_ref, g0_ref, g1_ref, acc_ref):
    s = pl.program_id(0)
    sb = jnp.maximum(s - 1, 0)
    sc = jnp.maximum(s - 2, 0)
    e = sb % ne
    eb = u_ref.shape[0]
    tt = ht_ref.shape[1]
    d = vt_ref.shape[0]
    ni = eb // nk
    assert ni == SUBLANES and ni % PEER_PARTS == 0, (eb, nk)
    grp = nk // SUBLANES
    sqrt_half = math.sqrt(0.5)

    @pl.when(s == 0)
    def _():
        for r in (at0_ref, at1_ref, g0_ref, g1_ref):
            r[...] = jnp.zeros_like(r)

    @pl.when(sc % ne == 0)
    def _():
        acc_ref[...] = jnp.zeros_like(acc_ref)

    @pl.when(e == 0)
    def _():
        for h in range(nheads):
            m1 = stats_ref[h, 1:2, :]
            thr = stats_ref[h, 0:1, :]
            for g in range(grp):
                s1 = st_ref[2 * h * nk + g * SUBLANES:2 * h * nk + (g + 1) * SUBLANES, :]
                c1_ref[h * grp + g] = thr - s1
                e1_ref[h * grp + g] = jnp.exp(s1 - m1)
            s2 = st_ref[(2 * h + 1) * nk:(2 * h + 2) * nk, :]
            e2_ref[h * nk:(h + 1) * nk, :] = jnp.exp(s2 - stats_ref[h, 2:3, :]) * (0.5 * stats_ref[h, 3:4, :])

    assert tt // LANES == PEER_PARTS
    mh_rows = eb // 2
    dh_rows = d // 2
    nh_cols = tt // 2

    def part(at_w, at_r, g_w, g_r, k, _):
        mh = k // 2
        nc = pl.ds(pl.multiple_of((k % 2) * nh_cols, nh_cols), nh_cols)
        er = pl.ds(pl.multiple_of(mh * mh_rows, mh_rows), mh_rows)
        at_w[er, nc] = _dot(u_ref[er, :], ht_ref[:, nc])
        cs = pl.ds(pl.multiple_of(k * LANES, LANES), LANES)
        for j0 in range(0, nk, PEER_ROWS):
            w = [jnp.zeros((PEER_ROWS, LANES), F32) for _ in range(ni)]
            for h in range(nheads):
                s2 = st_ref[(2 * h + 1) * nk + j0:(2 * h + 1) * nk + j0 + PEER_ROWS, cs]
                e2 = e2_ref[h * nk + j0:h * nk + j0 + PEER_ROWS, cs]
                for ii in range(ni):
                    sel = s2 >= c1_ref[h * grp + e, ii:ii + 1, cs]
                    w[ii] = w[ii] + e1_ref[h * grp + e, ii:ii + 1, cs] * jnp.where(sel, e2, 0.0)
            for ii in range(ni):
                rows = slice(ii * nk + j0, ii * nk + j0 + PEER_ROWS)
                a = at_r[rows, cs]
                g_w[rows, cs] = ((w[ii] * a) * (1.0 + lax.erf(a * sqrt_half))).astype(BF16)
        dr = pl.ds(pl.multiple_of(mh * dh_rows, dh_rows), dh_rows)
        acc_ref[dr, nc] += _dot(vt_ref[dr, :], g_r[:, nc])
        return 0

    @pl.when(s % 2 == 0)
    def _():
        lax.fori_loop(0, PEER_PARTS, functools.partial(part, at0_ref, at1_ref, g1_ref, g0_ref), 0)

    @pl.when(s % 2 == 1)
    def _():
        lax.fori_loop(0, PEER_PARTS, functools.partial(part, at1_ref, at0_ref, g0_ref, g1_ref), 0)

    @pl.when(jnp.logical_and(s >= 2, sc % ne == ne - 1))
    def _():
        out_ref[...] = acc_ref[...].T


def _k5_call(h2t, st, stats, u_all, layer, vt, nheads, nk, tt, eb):
    d, n = h2t.shape
    ne = u_all.shape[1] // eb
    nt = n // tt
    tile_a = lambda s: jnp.minimum(s // ne, nt - 1)
    tile_b = lambda s: jnp.minimum(jnp.maximum(s - 1, 0) // ne, nt - 1)
    tile_c = lambda s: jnp.minimum(jnp.maximum(s - 2, 0) // ne, nt - 1)
    return pl.pallas_call(
        functools.partial(_k5_kernel, nheads, nk, ne),
        out_shape=jax.ShapeDtypeStruct((n, d), F32),
        grid=(nt * ne + 2,),
        in_specs=[pl.BlockSpec((d, tt), lambda s: (0, tile_a(s))),
                  pl.BlockSpec((st.shape[0], tt), lambda s: (0, tile_b(s))),
                  pl.BlockSpec((nheads, SUBLANES, tt), lambda s: (0, 0, tile_b(s))),
                  pl.BlockSpec((None, eb, d), lambda s: (layer, s % ne, 0)),
                  pl.BlockSpec((d, eb), lambda s: (0, jnp.maximum(s - 2, 0) % ne))],
        out_specs=pl.BlockSpec((tt, d), lambda s: (tile_c(s), 0)),
        scratch_shapes=[pltpu.VMEM((nheads * nk // SUBLANES, SUBLANES, tt), F32),
                        pltpu.VMEM((nheads * nk // SUBLANES, SUBLANES, tt), F32),
                        pltpu.VMEM((nheads * nk, tt), F32),
                        pltpu.VMEM((eb, tt), F32), pltpu.VMEM((eb, tt), F32),
                        pltpu.VMEM((eb, tt), BF16), pltpu.VMEM((eb, tt), BF16),
                        pltpu.VMEM((d, tt), F32)],
        compiler_params=_cparams(("arbitrary",)),
        name="peer_dense",
    )(h2t, st, stats, u_all, vt)


def _res_kernel(x_ref, po_ref, gt_ref, o_ref):
    o_ref[0] = x_ref[0] + gt_ref[0] * po_ref[0]


def _res_call(x, po, gt, tm, per_token):
    bsz, seq, d = x.shape
    return pl.pallas_call(
        _res_kernel,
        out_shape=jax.ShapeDtypeStruct((bsz, seq, d), F32),
        grid=(bsz, seq // tm),
        in_specs=[_tok_spec(tm, d), _tok_spec(tm, d), _mod_spec(per_token, tm, d)],
        out_specs=_tok_spec(tm, d),
        compiler_params=_cparams(("arbitrary", "arbitrary")),
        name="final_residual",
    )(x, po, gt)


def _t5_bucket(rel):
    nb = REL_BUCKETS // 2
    max_exact = nb // 2
    side = jnp.where(rel > 0, nb, 0)
    n = jnp.abs(rel)
    nf = jnp.maximum(n, 1).astype(jnp.float32)
    large = max_exact + (jnp.log(nf / max_exact) / math.log(REL_MAX_DIST / max_exact)
                         * (nb - max_exact)).astype(jnp.int32)
    large = jnp.minimum(large, nb - 1)
    return side + jnp.where(n < max_exact, n, large)


def _bias_table(rel_bias, q_pos, k_pos):
    bucket = _t5_bucket(k_pos[None, :] - q_pos[:, None])
    rb = rel_bias.astype(F32)
    out = jnp.zeros((rb.shape[1],) + bucket.shape, F32)
    for b in range(rb.shape[0]):
        out = jnp.where(bucket[None] == b, rb[b][:, None, None], out)
    return out


def _block_diag(w):
    g, a, b = w.shape
    eye = jnp.eye(g, dtype=w.dtype)
    return (eye[:, None, :, None] * w[:, :, None, :]).reshape(g * a, g * b)


def _trunk(x, c_mod, per_token, sample, layer_w, rel_bias, tm, tq, tt, eb):
    bsz, seq, d = x.shape
    depth = len(layer_w)
    k_out, v_out, pool_out, gv_out = [], [], [], []
    tok = (lambda t: t.reshape(1, bsz * seq, t.shape[-1])) if per_token else (lambda t: t)
    seqv = lambda t: t.reshape(bsz, seq, t.shape[-1])
    xt = tok(x)
    res = None
    if sample is None:
        pos = jnp.arange(tq)
        tiles = jnp.stack([_bias_table(rel_bias, pos, pos - dlt * tq) for dlt in range(3)], axis=1)
    else:
        past = sample[1].shape[2]
        q_pos = past + jnp.arange(seq)
        bias_c = _bias_table(rel_bias, q_pos, jnp.arange(past))
        bias_n = jnp.pad(_bias_table(rel_bias, q_pos, q_pos), ((0, 0), (0, 0), (0, LANES - seq)),
                         constant_values=NEG_BIG)
    for l in range(depth):
        w = layer_w[l]
        sh1, sc1, gt1, sh2, sc2, gt2 = c_mod[l]
        heads = w["heads"]
        widths = w["widths"]
        outs = _k1_call(xt, res, sh1, sc1, w["g1"], w["w_in"], w["p256"], w["g_gv"],
                        w["gq"], w["gk"], widths, tm, per_token)
        if res is not None:
            xt, outs = outs[0], outs[1:]
        za, zu, vn, qn, kn, v = [seqv(t) for t in outs]
        if sample is None:
            hist = jnp.zeros((bsz, POOL_CARRY, za.shape[2]), F32)
            ya, yb = _k2_call(za, hist, zu, vn, w["wp_bd"], w["s_pool"], w["w_s"], w["bsp"], 0, 2 * GMLP_CHUNK)
            pool_out.append(za[:, seq - (POOL_CARRY - 1):])
        else:
            state_pool, cache_k, cache_v = sample
            hist = jnp.pad(state_pool[l], ((0, 0), (1, 0), (0, 0)))
            padr = ((0, 0), (0, GMLP_CHUNK - seq), (0, 0))
            ya, yb = _k2_call(jnp.pad(za, padr), hist, jnp.pad(zu, padr), jnp.pad(vn, padr), w["wp_bd"],
                              w["s_pool"], w["w_s"], w["bsp"], cache_k.shape[2], GMLP_CHUNK)
            ya, yb = ya[:, :seq], yb[:, :seq]
            pool_out.append(jnp.concatenate([state_pool[l], za], axis=1)[:, -(POOL_CARRY - 1):])
            gv_out.append(vn)
        lam_init = 0.8 - 0.6 * math.exp(-0.3 * l)
        if sample is None:
            o = _attn_prompt_call(qn, kn, v, tiles, w["lamv"], w["g_sub"], lam_init, heads, tq)
        else:
            o = _attn_sample_call(qn, cache_k, cache_v, l, kn, v, bias_c, bias_n, w["lamv"], w["g_sub"],
                                  lam_init, heads)
        k_out.append(kn.reshape(bsz, seq, heads, -1))
        v_out.append(v.reshape(bsz, seq, heads, -1))
        xt, st, h2t = _k3_call(xt, tok(ya), tok(yb), tok(o), gt1, sh2, sc2, w["g2"], w["w_out"], w["w_pq"],
                               w["keys"], tm, per_token)
        nlist = w["keys"].shape[0]
        nk = w["keys"].shape[1]
        stats = _k4_call(st, nlist, tt)
        po = _k5_call(h2t, st, stats, w["u_all"], l, w["vt"], nlist // 2, nk, tt, eb)
        res = (po.reshape(xt.shape), gt2)
    y = _res_call(xt, res[0], res[1], tm, per_token).reshape(bsz, seq, d)
    return y, jnp.stack(k_out), jnp.stack(v_out), jnp.stack(pool_out), (jnp.stack(gv_out) if gv_out else None)


def kernel(x_prompt, x_sample, c_prompt, c_sample, cache_k, cache_v, state_pool, rel_bias, w_ada, b_ada,
           g_norm, w_in, w_out, w_pool, s_pool, g_gv, w_s, b_s, g_qk, lam_vecs, g_sub, w_pq, sub_keys,
           u_tab, v_tab):
    depth = w_in.shape[0]
    d = x_prompt.shape[2]
    bp, sp = x_prompt.shape[:2]
    bs, ss = x_sample.shape[:2]
    heads = cache_k.shape[3]
    dqk = cache_k.shape[4]
    dv = cache_v.shape[4]
    wa = w_pool.shape[1] * w_pool.shape[2]
    wg = g_gv.shape[1]
    widths = (wa, wg, heads * dqk, heads * dv)
    grp = dqk // 2

    p256 = _block_diag(jnp.full((256 // grp, grp, grp), 1.0 / grp, F32)).astype(BF16)
    c_all = jnp.concatenate([c_prompt, c_sample], axis=0)
    layer_w, mods_p, mods_s = [], [], []
    for l in range(depth):
        mod = _ada_call(c_all, w_ada[l], b_ada[l])
        six = jnp.split(mod, 6, axis=-1)
        mods_p.append([m[:bp, None, :] for m in six])
        mods_s.append([jnp.broadcast_to(m[bp:, None, :], (bs, ss, d)).reshape(1, bs * ss, d) for m in six])
        nh, two, nk, kd = sub_keys.shape[1:]
        layer_w.append(dict(
            heads=heads, widths=widths,
            g1=g_norm[l, 0].reshape(1, d), g2=g_norm[l, 1].reshape(1, d),
            w_in=w_in[l].astype(BF16), w_out=w_out[l].astype(BF16), w_pq=w_pq[l].astype(BF16),
            p256=p256, g_gv=g_gv[l].reshape(1, wg),
            gq=jnp.tile(g_qk[l, 0], heads * dqk // grp).reshape(1, heads * dqk),
            gk=jnp.tile(g_qk[l, 1], heads * dqk // grp).reshape(1, heads * dqk),
            wp_bd=_block_diag(w_pool[l]).astype(BF16), s_pool=s_pool[l].reshape(1, wa),
            w_s=w_s[l], bsp=jnp.repeat(b_s[l].T, wg // b_s.shape[1], axis=1),
            lamv=lam_vecs[l], g_sub=g_sub[l].reshape(heads, 1, dv),
            keys=sub_keys[l].reshape(nh * two, nk, kd).astype(BF16),
            u_all=u_tab, vt=v_tab[l].T))

    tm = min(256, sp)
    y_p, k_p, v_p, pool_p, _ = _trunk(x_prompt, mods_p, False, None, layer_w, rel_bias,
                                      tm=tm, tq=min(512, sp), tt=min(512, bp * sp), eb=SUBLANES * sub_keys.shape[3])
    y_s, k_s, v_s, pool_s, gv_s = _trunk(x_sample, mods_s, True, (state_pool, cache_k, cache_v), layer_w,
                                         rel_bias, tm=min(256, bs * ss), tq=None, tt=min(512, bs * ss), eb=SUBLANES * sub_keys.shape[3])
    return (y_p, y_s, k_p, v_p, pool_p, k_s, v_s, pool_s, gv_s)
```

```python
import functools
import math

import jax
import jax.numpy as jnp
from jax import lax
from jax.experimental import pallas as pl
from jax.experimental.pallas import tpu as pltpu

F32 = jnp.float32
BF16 = jnp.bfloat16

EPS = 1e-6
ATT_CHUNK = 64
POOL_WINDOWS = (2, 4, 8, 16)
POOL_CARRY = 16
GMLP_CHUNK = 128
REL_BUCKETS = 32
REL_MAX_DIST = 128
PEER_TOPK = 16
NEG_BIG = -1e30

LANES = 128
SUBLANES = 8
VMEM_LIMIT = 56 * 1024 * 1024


def _cparams(sem, flags=None):
    return pltpu.CompilerParams(dimension_semantics=sem, vmem_limit_bytes=VMEM_LIMIT, flags=flags)


def _dot(a, b):
    return lax.dot_general(a, b, (((1,), (0,)), ((), ())), preferred_element_type=F32)


def _idiv_pow2(x, n):
    assert n & (n - 1) == 0, n
    return jnp.right_shift(x, n.bit_length() - 1)


def _dot_nt(a, b):
    return lax.dot_general(a, b, (((1,), (1,)), ((), ())), preferred_element_type=F32)


def _ada_kernel(c_ref, w_ref, b_ref, o_ref):
    c = c_ref[...]
    s = c * jax.nn.sigmoid(c)
    o_ref[...] = _dot(s, w_ref[...]) + b_ref[...]


def _ada_call(c, w, b):
    m, d = c.shape
    n = w.shape[1]
    nb = 1536
    return pl.pallas_call(
        _ada_kernel,
        out_shape=jax.ShapeDtypeStruct((m, n), F32),
        grid=(n // nb,),
        in_specs=[pl.BlockSpec((m, d), lambda j: (0, 0)),
                  pl.BlockSpec((d, nb), lambda j: (0, j)),
                  pl.BlockSpec((1, nb), lambda j: (0, j))],
        out_specs=pl.BlockSpec((m, nb), lambda j: (0, j)),
        compiler_params=_cparams(("arbitrary",)),
        name="ada_mod",
    )(c, w, b.reshape(1, n))


def _rms_rows(x):
    return x * lax.rsqrt(jnp.mean(x * x, axis=-1, keepdims=True) + EPS)


def _group_norm_256(t, p_ref):
    pieces = []
    for s in range(t.shape[1] // 256):
        ts = t[:, s * 256:(s + 1) * 256]
        msq = _dot((ts * ts).astype(BF16), p_ref[...])
        pieces.append(ts * lax.rsqrt(msq + EPS))
    return pieces[0] if len(pieces) == 1 else jnp.concatenate(pieces, axis=-1)


def _k1_kernel(widths, x_ref, sh_ref, sc_ref, g_ref, win_ref, p_ref, gv_ref, gq_ref, gk_ref,
               za_ref, zu_ref, vn_ref, q_ref, k_ref, v_ref):
    h = _rms_rows(x_ref[0]) * g_ref[...] * (1.0 + sc_ref[0]) + sh_ref[0]
    z = _dot(h.astype(BF16), win_ref[...])
    wa, wg, wqk, wv = widths
    o = 0
    za_ref[0] = z[:, o:o + wa]; o += wa
    zu_ref[0] = z[:, o:o + wg]; o += wg
    vn_ref[0] = _group_norm_256(z[:, o:o + wg], p_ref) * gv_ref[...]; o += wg
    q_ref[0] = _group_norm_256(z[:, o:o + wqk], p_ref) * gq_ref[...]; o += wqk
    k_ref[0] = _group_norm_256(z[:, o:o + wqk], p_ref) * gk_ref[...]; o += wqk
    v_ref[0] = z[:, o:o + wv]


def _tok_spec(tm, d):
    return pl.BlockSpec((1, tm, d), lambda b, s: (b, s, 0))


def _mod_spec(per_token, tm, d):
    if per_token:
        return pl.BlockSpec((1, tm, d), lambda b, s: (b, s, 0))
    return pl.BlockSpec((1, 1, d), lambda b, s: (b, 0, 0))


def _full_spec(shape):
    nd = len(shape)
    return pl.BlockSpec(shape, lambda b, s: (0,) * nd)


def _k1_call(x, sh, sc, g, w_in, p256, gv, gq, gk, widths, tm, per_token):
    bsz, seq, d = x.shape
    wa, wg, wqk, wv = widths
    ins = [x, sh, sc, g, w_in, p256, gv, gq, gk]
    specs = [_tok_spec(tm, d), _mod_spec(per_token, tm, d), _mod_spec(per_token, tm, d), _full_spec(g.shape),
             _full_spec(w_in.shape), _full_spec(p256.shape), _full_spec(gv.shape),
             _full_spec(gq.shape), _full_spec(gk.shape)]
    out_w = [wa, wg, wg, wqk, wqk, wv]
    return pl.pallas_call(
        functools.partial(_k1_kernel, widths),
        out_shape=[jax.ShapeDtypeStruct((bsz, seq, w), F32) for w in out_w],
        grid=(bsz, seq // tm),
        in_specs=specs,
        out_specs=[_tok_spec(tm, w) for w in out_w],
        compiler_params=_cparams(("arbitrary", "arbitrary")),
        name="in_proj",
    )(*ins)


def _k2_kernel(pos0, za_ref, hist_ref, zu_ref, vn_ref, wp_ref, sp_ref, ws_ref, bsp_ref,
               ya_ref, yb_ref, carry_ref):
    s = pl.program_id(1)
    tb = za_ref.shape[1]
    w = za_ref.shape[2]
    gw = w // len(POOL_WINDOWS)

    @pl.when(s == 0)
    def _():
        carry_ref[...] = hist_ref[0]

    za = za_ref[0]
    ext = jnp.concatenate([carry_ref[...], za], axis=0)
    carry_ref[...] = za[tb - POOL_CARRY:, :]
    sums = {1: ext}
    d = 1
    while d < max(POOL_WINDOWS):
        sums[2 * d] = sums[d] + pltpu.roll(sums[d], d, axis=0)
        d *= 2
    row = lax.broadcasted_iota(jnp.int32, (tb, w), 0)
    lane_grp = _idiv_pow2(lax.broadcasted_iota(jnp.int32, (tb, w), 1), gw)
    pos1 = (pos0 + s * tb + row + 1).astype(F32)
    means = None
    for gi, win in enumerate(POOL_WINDOWS):
        m = sums[win][POOL_CARRY:, :] / jnp.minimum(float(win), pos1)
        means = m if means is None else jnp.where(lane_grp == gi, m, means)
    dlt = means - za
    ya_ref[0] = _dot(dlt.astype(BF16), wp_ref[...]) * sp_ref[...]

    ll = ws_ref.shape[1]
    r_i = lax.broadcasted_iota(jnp.int32, (ll, ll), 0)
    c_i = lax.broadcasted_iota(jnp.int32, (ll, ll), 1)
    grp = _idiv_pow2(lax.broadcasted_iota(jnp.int32, (ll, w), 1), w // ws_ref.shape[0])
    for c in range(tb // ll):
        vnc = vn_ref[0, c * ll:(c + 1) * ll, :].astype(BF16)
        spv = None
        for gi in range(ws_ref.shape[0]):
            wsg = jnp.where(r_i >= c_i, ws_ref[gi], 0.0).astype(BF16)
            t = _dot(wsg, vnc)
            spv = t if spv is None else jnp.where(grp == gi, t, spv)
        yb_ref[0, c * ll:(c + 1) * ll, :] = zu_ref[0, c * ll:(c + 1) * ll, :] * (spv + bsp_ref[...])


def _k2_call(za, hist16, zu, vn, wp_bd, s_pool, ws, bsp, pos0, tb):
    bsz, seq, w = za.shape
    return pl.pallas_call(
        functools.partial(_k2_kernel, pos0),
        out_shape=[jax.ShapeDtypeStruct((bsz, seq, w), F32)] * 2,
        grid=(bsz, seq // tb),
        in_specs=[_tok_spec(tb, w),
                  pl.BlockSpec((1, POOL_CARRY, w), lambda b, s: (b, 0, 0)),
                  _tok_spec(tb, w), _tok_spec(tb, w),
                  _full_spec(wp_bd.shape), _full_spec(s_pool.shape), _full_spec(ws.shape),
                  _full_spec(bsp.shape)],
        out_specs=[_tok_spec(tb, w)] * 2,
        scratch_shapes=[pltpu.VMEM((POOL_CARRY, w), F32)],
        compiler_params=_cparams(("arbitrary", "arbitrary")),
        name="pool_gmlp",
    )(za, hist16, zu, vn, wp_bd, s_pool, ws, bsp)


def _lambda(lamv_ref, lam_init):
    lv = lamv_ref[...]
    a = jnp.sum(lv[0:1] * lv[1:2], axis=-1, keepdims=True)
    b = jnp.sum(lv[2:3] * lv[3:4], axis=-1, keepdims=True)
    return jnp.exp(a) - jnp.exp(b) + lam_init


def _split_maps(q):
    lane = lax.broadcasted_iota(jnp.int32, q.shape, 1)
    half = q.shape[1] // 2
    return jnp.where(lane < half, q, 0.0), jnp.where(lane >= half, q, 0.0)


def _online_update(state, s, vblk):
    m, l, acc = state
    m_new = jnp.maximum(m, jnp.max(s, axis=-1, keepdims=True))
    alpha = jnp.exp(m - m_new)
    p = jnp.exp(s - m_new)
    l_new = alpha * l + jnp.sum(p, axis=-1, keepdims=True)
    acc_new = alpha * acc + _dot(p, vblk)
    return m_new, l_new, acc_new


def _sub_norm(o0, o1, lam, lam_init, gsub):
    o = o0 - lam * o1
    return _rms_rows(o) * gsub * (1.0 - lam_init)


def _attn_prompt_kernel(lam_init, scale, q_ref, k_ref, v_ref, bias_ref, lamv_ref, gsub_ref, o_ref):
    qb = pl.program_id(2)
    tq = q_ref.shape[1]
    dv = v_ref.shape[2]
    q0, q1 = _split_maps(q_ref[0] * scale)

    def init():
        return (jnp.full((tq, 1), NEG_BIG, F32), jnp.zeros((tq, 1), F32), jnp.zeros((tq, dv), F32))

    def block(kb, bias):
        kblk = k_ref[0, pl.ds(pl.multiple_of(kb * tq, tq), tq), :].astype(BF16)
        vblk = v_ref[0, pl.ds(pl.multiple_of(kb * tq, tq), tq), :].astype(BF16)
        return _dot_nt(q0, kblk) + bias, _dot_nt(q1, kblk) + bias, vblk

    def body(kb, carry):
        st0, st1 = carry
        bias = bias_ref[0, jnp.minimum(qb - kb, 2)]
        s0, s1, vblk = block(kb, bias)
        return _online_update(st0, s0, vblk), _online_update(st1, s1, vblk)

    st0, st1 = lax.fori_loop(0, qb, body, (init(), init()))
    s0, s1, vblk = block(qb, bias_ref[0, 0])
    qi = _idiv_pow2(lax.broadcasted_iota(jnp.int32, (tq, tq), 0), ATT_CHUNK)
    ki = _idiv_pow2(lax.broadcasted_iota(jnp.int32, (tq, tq), 1), ATT_CHUNK)
    vis = ki <= qi
    st0 = _online_update(st0, jnp.where(vis, s0, NEG_BIG), vblk)
    st1 = _online_update(st1, jnp.where(vis, s1, NEG_BIG), vblk)
    lam = _lambda(lamv_ref, lam_init)
    o_ref[0] = _sub_norm(st0[2] / st0[1], st1[2] / st1[1], lam, lam_init, gsub_ref[0])


def _attn_prompt_call(qn, kn, v, bias_tiles, lamv, gsub, lam_init, heads, tq):
    bsz, seq, _ = qn.shape
    dh = qn.shape[2] // heads
    dv = v.shape[2] // heads
    scale = float((dh // 2) ** -0.5)
    return pl.pallas_call(
        functools.partial(_attn_prompt_kernel, lam_init, scale),
        out_shape=jax.ShapeDtypeStruct((bsz, seq, heads * dv), F32),
        grid=(bsz, heads, seq // tq),
        in_specs=[pl.BlockSpec((1, tq, dh), lambda b, h, i: (b, i, h)),
                  pl.BlockSpec((1, seq, dh), lambda b, h, i: (b, 0, h)),
                  pl.BlockSpec((1, seq, dv), lambda b, h, i: (b, 0, h)),
                  pl.BlockSpec((1, 3, tq, tq), lambda b, h, i: (h, 0, 0, 0)),
                  pl.BlockSpec(lamv.shape, lambda b, h, i: (0, 0)),
                  pl.BlockSpec((1, 1, dv), lambda b, h, i: (h, 0, 0))],
        out_specs=pl.BlockSpec((1, tq, dv), lambda b, h, i: (b, i, h)),
        compiler_params=_cparams(("arbitrary", "arbitrary", "arbitrary")),
        name="attn_prompt",
    )(qn, kn, v, bias_tiles, lamv, gsub)


def _attn_sample_kernel(lam_init, scale, heads, q_ref, kc_ref, vc_ref, kn_ref, vn_ref,
                        bc_ref, bn_ref, lamv_ref, gsub_ref, o_ref):
    t = q_ref.shape[1]
    dh = q_ref.shape[2] // heads
    dv = vc_ref.shape[3]
    lam = _lambda(lamv_ref, lam_init)
    pad = bn_ref.shape[2] - t
    for h in range(heads):
        q0, q1 = _split_maps(q_ref[0, :, h * dh:(h + 1) * dh] * scale)
        kc = kc_ref[0, :, h, :].astype(BF16)
        vc = vc_ref[0, :, h, :].astype(BF16)
        kn = jnp.concatenate([kn_ref[0, :, h * dh:(h + 1) * dh], jnp.zeros((pad, dh), F32)], axis=0).astype(BF16)
        vn = jnp.concatenate([vn_ref[0, :, h * dv:(h + 1) * dv], jnp.zeros((pad, dv), F32)], axis=0).astype(BF16)
        outs = []
        for qm in (q0, q1):
            sc = _dot_nt(qm, kc) + bc_ref[h]
            sn = _dot_nt(qm, kn) + bn_ref[h]
            m = jnp.maximum(jnp.max(sc, axis=-1, keepdims=True), jnp.max(sn, axis=-1, keepdims=True))
            pc = jnp.exp(sc - m)
            pn = jnp.exp(sn - m)
            l = jnp.sum(pc, axis=-1, keepdims=True) + jnp.sum(pn, axis=-1, keepdims=True)
            outs.append((_dot(pc, vc) + _dot(pn, vn)) / l)
        o_ref[0, :, h * dv:(h + 1) * dv] = _sub_norm(outs[0], outs[1], lam, lam_init, gsub_ref[h])


def _attn_sample_call(qn, cache_k, cache_v, layer, kn, vn, bias_c, bias_n, lamv, gsub, lam_init, heads):
    bsz, t, wq = qn.shape
    wv = heads * cache_v.shape[4]
    dh = wq // heads
    scale = float((dh // 2) ** -0.5)
    row = lambda shape: pl.BlockSpec((1,) + shape[1:], lambda b: (b, 0, 0))
    full = lambda shape: pl.BlockSpec(shape, lambda b: (0,) * len(shape))
    cache = lambda shape: pl.BlockSpec((None, 1) + shape[2:], lambda b: (layer, b, 0, 0, 0))
    return pl.pallas_call(
        functools.partial(_attn_sample_kernel, lam_init, scale, heads),
        out_shape=jax.ShapeDtypeStruct((bsz, t, wv), F32),
        grid=(bsz,),
        in_specs=[row(qn.shape), cache(cache_k.shape), cache(cache_v.shape), row(kn.shape), row(vn.shape),
                  full(bias_c.shape), full(bias_n.shape), full(lamv.shape), full(gsub.shape)],
        out_specs=row((bsz, t, wv)),
        compiler_params=_cparams(("arbitrary",)),
        name="attn_sample",
    )(qn, cache_k, cache_v, kn, vn, bias_c, bias_n, lamv, gsub)


def _k3_kernel(x_ref, ya_ref, yb_ref, o_ref, gt_ref, sh_ref, sc_ref, g_ref, wout_ref, wpq_ref, keys_ref,
               x1_ref, st_ref, h2t_ref):
    wa = ya_ref.shape[2]
    wb = yb_ref.shape[2]
    y = (_dot(ya_ref[0].astype(BF16), wout_ref[0:wa, :])
         + _dot(yb_ref[0].astype(BF16), wout_ref[wa:wa + wb, :])
         + _dot(o_ref[0].astype(BF16), wout_ref[wa + wb:, :]))
    x1 = x_ref[0] + gt_ref[0] * y
    x1_ref[0] = x1
    h2 = _rms_rows(x1) * g_ref[...] * (1.0 + sc_ref[0]) + sh_ref[0]
    h2t_ref[...] = h2.T.astype(BF16)
    qp = _dot(h2.astype(BF16), wpq_ref[...])
    nk = keys_ref.shape[1]
    kd = keys_ref.shape[2]
    for r in range(keys_ref.shape[0]):
        st_ref[r * nk:(r + 1) * nk, :] = _dot_nt(keys_ref[r], qp[:, r * kd:(r + 1) * kd].astype(BF16))


def _k3_call(x, ya, yb, o, gt, sh, sc, g, w_out, w_pq, keys, tm, per_token):
    bsz, seq, d = x.shape
    n = bsz * seq
    ns = seq // tm
    nrow = keys.shape[0] * keys.shape[1]
    flat = lambda b, s: (0, b * ns + s)
    return pl.pallas_call(
        _k3_kernel,
        out_shape=[jax.ShapeDtypeStruct((bsz, seq, d), F32),
                   jax.ShapeDtypeStruct((nrow, n), F32),
                   jax.ShapeDtypeStruct((d, n), BF16)],
        grid=(bsz, ns),
        in_specs=[_tok_spec(tm, d), _tok_spec(tm, ya.shape[2]), _tok_spec(tm, yb.shape[2]),
                  _tok_spec(tm, o.shape[2]),
                  _mod_spec(per_token, tm, d), _mod_spec(per_token, tm, d), _mod_spec(per_token, tm, d),
                  _full_spec(g.shape), _full_spec(w_out.shape), _full_spec(w_pq.shape),
                  _full_spec(keys.shape)],
        out_specs=[_tok_spec(tm, d), pl.BlockSpec((nrow, tm), flat), pl.BlockSpec((d, tm), flat)],
        compiler_params=_cparams(("arbitrary", "arbitrary")),
        name="out_proj_peer_scores",
    )(x, ya, yb, o, gt, sh, sc, g, w_out, w_pq, keys)


def _sort_network(n):
    def merge(lo, hi, r):
        step = r * 2
        if step < hi - lo:
            yield from merge(lo, hi, step)
            yield from merge(lo + r, hi, step)
            yield from ((i, i + r) for i in range(lo + r, hi - r, step))
        else:
            yield (lo, lo + r)

    def sort(lo, hi):
        if hi - lo >= 1:
            mid = lo + (hi - lo) // 2
            yield from sort(lo, mid)
            yield from sort(mid + 1, hi)
            yield from merge(lo, hi, 1)

    return list(sort(0, n - 1))


def _compare_exchange(vs, i, j):
    a, b = vs[i], vs[j]
    if b is None:
        return
    if a is None:
        vs[i], vs[j] = b, None
        return
    vs[i], vs[j] = jnp.maximum(a, b), jnp.minimum(a, b)


def _sort_desc(vs):
    vs = list(vs)
    for i, j in _sort_network(len(vs)):
        _compare_exchange(vs, i, j)
    return vs


def _top_across_sublanes(vs):
    n = len(vs)
    for shift in (4, 2, 1):
        mixed = []
        for k in range(n):
            a, b = vs[k], vs[n - 1 - k]
            if b is None:
                mixed.append(a)
            else:
                b = pltpu.roll(b, shift, axis=0)
                mixed.append(b if a is None else jnp.maximum(a, b))
        vs = mixed
        d = n // 2
        while d >= 1:
            for k in range(n):
                if k & d == 0:
                    _compare_exchange(vs, k, k + d)
            d //= 2
    return vs


def _all_sublanes(x, op):
    for shift in (4, 2, 1):
        x = op(x, pltpu.roll(x, shift, axis=0))
    return x


def _next_below(vals, cut, k):
    cnt = None
    below = None
    for v in vals:
        c = jnp.where(v >= cut, 1.0, 0.0)
        b = jnp.where(v < cut, v, -jnp.inf)
        cnt = c if cnt is None else cnt + c
        below = b if below is None else jnp.maximum(below, b)
    cnt = _all_sublanes(cnt, jnp.add)
    below = _all_sublanes(below, jnp.maximum)
    return jnp.where(cnt > float(k), cut, below)


PEER_TOP_ROWS = 24


def _k4_kernel(st_ref, stats_ref, top_ref):
    nlist = top_ref.shape[0]
    nk = st_ref.shape[0] // nlist
    k = PEER_TOPK
    assert nk // SUBLANES == k and k == 2 * SUBLANES

    def chunk(c, _):
        cs = pl.ds(pl.multiple_of(c * LANES, LANES), LANES)

        def lists(r, _):
            base = pl.multiple_of(r * nk, nk)
            vals = [st_ref[pl.ds(base + g * SUBLANES, SUBLANES), cs] for g in range(nk // SUBLANES)]
            top = _top_across_sublanes(_sort_desc(vals))
            for i in range(k):
                top_ref[r, i:i + 1, :] = top[i][0:1]
            top_ref[r, k:k + 1, :] = _next_below(vals, top[k - 1], k)[0:1]
            return 0

        lax.fori_loop(0, nlist, lists, 0)

        def heads(h, _):
            a_hi = top_ref[2 * h, SUBLANES:k]
            b_lo, b_hi = top_ref[2 * h + 1, 0:SUBLANES], top_ref[2 * h + 1, SUBLANES:k]
            a0, b0 = top_ref[2 * h, 0:1], top_ref[2 * h + 1, 0:1]
            cands = [a0 + b_lo, a0 + b_hi]
            cands += [top_ref[2 * h, p:p + 1] + b_lo for p in range(1, SUBLANES)]
            cands.append(a_hi + b0)
            tv = _top_across_sublanes(_sort_desc(cands + [None] * (k - len(cands))))
            v16 = tv[k - 1]
            v17 = _next_below(cands, v16, k)
            v17 = jnp.maximum(v17, jnp.maximum(top_ref[2 * h, k:k + 1] + b0, a0 + top_ref[2 * h + 1, k:k + 1]))
            z = jnp.ones_like(v16)
            for r in range(1, k):
                z = z + jnp.exp(tv[r] - tv[0])
            zero = jnp.zeros_like(a0)
            stats_ref[h, :, cs] = jnp.concatenate(
                [(0.5 * (v16 + v17))[0:1], a0, b0, (1.0 / z)[0:1], zero, zero, zero, zero], axis=0)
            return 0

        lax.fori_loop(0, nlist // 2, heads, 0)
        return 0

    lax.fori_loop(0, st_ref.shape[1] // LANES, chunk, 0)


def _k4_call(st, nlist, tt):
    nrow, n = st.shape
    nh = nlist // 2
    return pl.pallas_call(
        _k4_kernel,
        out_shape=jax.ShapeDtypeStruct((nh, SUBLANES, n), F32),
        grid=(n // tt,),
        in_specs=[pl.BlockSpec((nrow, tt), lambda i: (0, i))],
        out_specs=pl.BlockSpec((nh, SUBLANES, tt), lambda i: (0, 0, i)),
        scratch_shapes=[pltpu.VMEM((nlist, PEER_TOP_ROWS, LANES), F32)],
        compiler_params=_cparams(("arbitrary",)),
        name="peer_topk_stats",
    )(st)


PEER_ROWS = 32


PEER_PARTS = 2


def _k5_kernel(nheads, nk, ne, ht_ref, st_ref, stats_ref, u_ref, vt_ref, x_ref, gt_ref, out_ref,
               c1_ref, e1_ref, e2_ref, at0_ref, at1_ref, g0_ref, g1_ref, acc_ref):
    s = pl.program_id(0)
    sb = jnp.maximum(s - 1, 0)
    sc = jnp.maximum(s - 2, 0)
    e = sb % ne
    eb = u_ref.shape[0]
    tt = ht_ref.shape[1]
    d = vt_ref.shape[0]
    ni = eb // nk
    assert ni == SUBLANES, (eb, nk)
    grp = nk // SUBLANES
    sqrt_half = math.sqrt(0.5)

    @pl.when(s == 0)
    def _():
        for r in (at0_ref, at1_ref, g0_ref, g1_ref):
            r[...] = jnp.zeros_like(r)

    @pl.when(sc % ne == 0)
    def _():
        acc_ref[...] = jnp.zeros_like(acc_ref)

    @pl.when(e == 0)
    def _():
        for h in range(nheads):
            m1 = stats_ref[h, 1:2, :]
            thr = stats_ref[h, 0:1, :]
            for g in range(grp):
                s1 = st_ref[2 * h * nk + g * SUBLANES:2 * h * nk + (g + 1) * SUBLANES, :]
                c1_ref[h * grp + g] = thr - s1
                e1_ref[h * grp + g] = jnp.exp(s1 - m1)
            s2 = st_ref[(2 * h + 1) * nk:(2 * h + 2) * nk, :]
            e2_ref[h * nk:(h + 1) * nk, :] = jnp.exp(s2 - stats_ref[h, 2:3, :]) * (0.5 * stats_ref[h, 3:4, :])

    assert PEER_PARTS % 2 == 0 and (tt // LANES) % PEER_PARTS == 0
    msplit = PEER_PARTS // 2
    mh_rows = eb // msplit
    dh_rows = d // msplit
    nh_cols = tt // 2
    chunks = tt // LANES // PEER_PARTS

    def part(at_w, at_r, g_w, g_r, k, _):
        mh = k // 2
        nc = pl.ds(pl.multiple_of((k % 2) * nh_cols, nh_cols), nh_cols)
        er = pl.ds(pl.multiple_of(mh * mh_rows, mh_rows), mh_rows)
        at_w[er, nc] = _dot(u_ref[er, :], ht_ref[:, nc])
        for c in range(chunks):
            cs = pl.ds(pl.multiple_of((k * chunks + c) * LANES, LANES), LANES)
            for j0 in range(0, nk, PEER_ROWS):
                w = [jnp.zeros((PEER_ROWS, LANES), F32) for _ in range(ni)]
                for h in range(nheads):
                    s2 = st_ref[(2 * h + 1) * nk + j0:(2 * h + 1) * nk + j0 + PEER_ROWS, cs]
                    e2 = e2_ref[h * nk + j0:h * nk + j0 + PEER_ROWS, cs]
                    for ii in range(ni):
                        sel = s2 >= c1_ref[h * grp + e, ii:ii + 1, cs]
                        w[ii] = w[ii] + e1_ref[h * grp + e, ii:ii + 1, cs] * jnp.where(sel, e2, 0.0)
                for ii in range(ni):
                    rows = slice(ii * nk + j0, ii * nk + j0 + PEER_ROWS)
                    a = at_r[rows, cs]
                    g_w[rows, cs] = ((w[ii] * a) * (1.0 + lax.erf(a * sqrt_half))).astype(BF16)
        dr = pl.ds(pl.multiple_of(mh * dh_rows, dh_rows), dh_rows)
        acc_ref[dr, nc] += _dot(vt_ref[dr, :], g_r[:, nc])
        return 0

    @pl.when(s % 2 == 0)
    def _():
        lax.fori_loop(0, PEER_PARTS, functools.partial(part, at0_ref, at1_ref, g1_ref, g0_ref), 0)

    @pl.when(s % 2 == 1)
    def _():
        lax.fori_loop(0, PEER_PARTS, functools.partial(part, at1_ref, at0_ref, g0_ref, g1_ref), 0)

    @pl.when(jnp.logical_and(s >= 2, sc % ne == ne - 1))
    def _():
        out_ref[...] = x_ref[...] + gt_ref[0] * acc_ref[...].T


def _k5_call(h2t, st, stats, u_all, layer, vt, x, gt, per_token, nheads, nk, tt, eb):
    d, n = h2t.shape
    seq = x.shape[1]
    assert seq % tt == 0
    tiles_per_row = seq // tt
    ne = u_all.shape[1] // eb
    nt = n // tt
    tile_a = lambda s: jnp.minimum(s // ne, nt - 1)
    tile_b = lambda s: jnp.minimum(jnp.maximum(s - 1, 0) // ne, nt - 1)
    tile_c = lambda s: jnp.minimum(jnp.maximum(s - 2, 0) // ne, nt - 1)
    return pl.pallas_call(
        functools.partial(_k5_kernel, nheads, nk, ne),
        out_shape=jax.ShapeDtypeStruct((n, d), F32),
        grid=(nt * ne + 2,),
        in_specs=[pl.BlockSpec((d, tt), lambda s: (0, tile_a(s))),
                  pl.BlockSpec((st.shape[0], tt), lambda s: (0, tile_b(s))),
                  pl.BlockSpec((nheads, SUBLANES, tt), lambda s: (0, 0, tile_b(s))),
                  pl.BlockSpec((None, eb, d), lambda s: (layer, s % ne, 0)),
                  pl.BlockSpec((d, eb), lambda s: (0, jnp.maximum(s - 2, 0) % ne)),
                  pl.BlockSpec((tt, d), lambda s: (tile_c(s), 0)),
                  (pl.BlockSpec((1, tt, d), lambda s: (0, tile_c(s), 0)) if per_token else
                   pl.BlockSpec((1, 1, d), lambda s: (tile_c(s) // tiles_per_row, 0, 0)))],
        out_specs=pl.BlockSpec((tt, d), lambda s: (tile_c(s), 0)),
        scratch_shapes=[pltpu.VMEM((nheads * nk // SUBLANES, SUBLANES, tt), F32),
                        pltpu.VMEM((nheads * nk // SUBLANES, SUBLANES, tt), F32),
                        pltpu.VMEM((nheads * nk, tt), F32),
                        pltpu.VMEM((eb, tt), F32), pltpu.VMEM((eb, tt), F32),
                        pltpu.VMEM((eb, tt), BF16), pltpu.VMEM((eb, tt), BF16),
                        pltpu.VMEM((d, tt), F32)],
        compiler_params=_cparams(("arbitrary",)),
        name="peer_dense",
    )(h2t, st, stats, u_all, vt, x.reshape(n, d), gt)


def _t5_bucket(rel):
    nb = REL_BUCKETS // 2
    max_exact = nb // 2
    side = jnp.where(rel > 0, nb, 0)
    n = jnp.abs(rel)
    nf = jnp.maximum(n, 1).astype(jnp.float32)
    large = max_exact + (jnp.log(nf / max_exact) / math.log(REL_MAX_DIST / max_exact)
                         * (nb - max_exact)).astype(jnp.int32)
    large = jnp.minimum(large, nb - 1)
    return side + jnp.where(n < max_exact, n, large)


def _bias_table(rel_bias, q_pos, k_pos):
    bucket = _t5_bucket(k_pos[None, :] - q_pos[:, None])
    rb = rel_bias.astype(F32)
    out = jnp.zeros((rb.shape[1],) + bucket.shape, F32)
    for b in range(rb.shape[0]):
        out = jnp.where(bucket[None] == b, rb[b][:, None, None], out)
    return out


def _block_diag(w):
    g, a, b = w.shape
    eye = jnp.eye(g, dtype=w.dtype)
    return (eye[:, None, :, None] * w[:, :, None, :]).reshape(g * a, g * b)


def _trunk(x, c_mod, per_token, sample, layer_w, rel_bias, tm, tq, tt, eb):
    bsz, seq, d = x.shape
    depth = len(layer_w)
    k_out, v_out, pool_out, gv_out = [], [], [], []
    tok = (lambda t: t.reshape(1, bsz * seq, t.shape[-1])) if per_token else (lambda t: t)
    seqv = lambda t: t.reshape(bsz, seq, t.shape[-1])
    xt = tok(x)
    if sample is None:
        pos = jnp.arange(tq)
        tiles = jnp.stack([_bias_table(rel_bias, pos, pos - dlt * tq) for dlt in range(3)], axis=1)
    else:
        past = sample[1].shape[2]
        q_pos = past + jnp.arange(seq)
        bias_c = _bias_table(rel_bias, q_pos, jnp.arange(past))
        bias_n = jnp.pad(_bias_table(rel_bias, q_pos, q_pos), ((0, 0), (0, 0), (0, LANES - seq)),
                         constant_values=NEG_BIG)
    for l in range(depth):
        w = layer_w[l]
        sh1, sc1, gt1, sh2, sc2, gt2 = c_mod[l]
        heads = w["heads"]
        widths = w["widths"]
        outs = _k1_call(xt, sh1, sc1, w["g1"], w["w_in"], w["p256"], w["g_gv"],
                        w["gq"], w["gk"], widths, tm, per_token)
        za, zu, vn, qn, kn, v = [seqv(t) for t in outs]
        if sample is None:
            hist = jnp.zeros((bsz, POOL_CARRY, za.shape[2]), F32)
            ya, yb = _k2_call(za, hist, zu, vn, w["wp_bd"], w["s_pool"], w["w_s"], w["bsp"], 0, 2 * GMLP_CHUNK)
            pool_out.append(za[:, seq - (POOL_CARRY - 1):])
        else:
            state_pool, cache_k, cache_v = sample
            hist = jnp.pad(state_pool[l], ((0, 0), (1, 0), (0, 0)))
            padr = ((0, 0), (0, GMLP_CHUNK - seq), (0, 0))
            ya, yb = _k2_call(jnp.pad(za, padr), hist, jnp.pad(zu, padr), jnp.pad(vn, padr), w["wp_bd"],
                              w["s_pool"], w["w_s"], w["bsp"], cache_k.shape[2], GMLP_CHUNK)
            ya, yb = ya[:, :seq], yb[:, :seq]
            pool_out.append(jnp.concatenate([state_pool[l], za], axis=1)[:, -(POOL_CARRY - 1):])
            gv_out.append(vn)
        lam_init = 0.8 - 0.6 * math.exp(-0.3 * l)
        if sample is None:
            o = _attn_prompt_call(qn, kn, v, tiles, w["lamv"], w["g_sub"], lam_init, heads, tq)
        else:
            o = _attn_sample_call(qn, cache_k, cache_v, l, kn, v, bias_c, bias_n, w["lamv"], w["g_sub"],
                                  lam_init, heads)
        k_out.append(kn.reshape(bsz, seq, heads, -1))
        v_out.append(v.reshape(bsz, seq, heads, -1))
        xt, st, h2t = _k3_call(xt, tok(ya), tok(yb), tok(o), gt1, sh2, sc2, w["g2"], w["w_out"], w["w_pq"],
                               w["keys"], tm, per_token)
        nlist = w["keys"].shape[0]
        nk = w["keys"].shape[1]
        stats = _k4_call(st, nlist, tt)
        xt = _k5_call(h2t, st, stats, w["u_all"], l, w["vt"], xt, gt2, per_token, nlist // 2, nk, tt,
                      eb).reshape(xt.shape)
    y = xt.reshape(bsz, seq, d)
    return y, jnp.stack(k_out), jnp.stack(v_out), jnp.stack(pool_out), (jnp.stack(gv_out) if gv_out else None)


def kernel(x_prompt, x_sample, c_prompt, c_sample, cache_k, cache_v, state_pool, rel_bias, w_ada, b_ada,
           g_norm, w_in, w_out, w_pool, s_pool, g_gv, w_s, b_s, g_qk, lam_vecs, g_sub, w_pq, sub_keys,
           u_tab, v_tab):
    depth = w_in.shape[0]
    d = x_prompt.shape[2]
    bp, sp = x_prompt.shape[:2]
    bs, ss = x_sample.shape[:2]
    heads = cache_k.shape[3]
    dqk = cache_k.shape[4]
    dv = cache_v.shape[4]
    wa = w_pool.shape[1] * w_pool.shape[2]
    wg = g_gv.shape[1]
    widths = (wa, wg, heads * dqk, heads * dv)
    grp = dqk // 2

    p256 = _block_diag(jnp.full((256 // grp, grp, grp), 1.0 / grp, F32)).astype(BF16)
    c_all = jnp.concatenate([c_prompt, c_sample], axis=0)
    layer_w, mods_p, mods_s = [], [], []
    for l in range(depth):
        mod = _ada_call(c_all, w_ada[l], b_ada[l])
        six = jnp.split(mod, 6, axis=-1)
        mods_p.append([m[:bp, None, :] for m in six])
        mods_s.append([jnp.broadcast_to(m[bp:, None, :], (bs, ss, d)).reshape(1, bs * ss, d) for m in six])
        nh, two, nk, kd = sub_keys.shape[1:]
        layer_w.append(dict(
            heads=heads, widths=widths,
            g1=g_norm[l, 0].reshape(1, d), g2=g_norm[l, 1].reshape(1, d),
            w_in=w_in[l].astype(BF16), w_out=w_out[l].astype(BF16), w_pq=w_pq[l].astype(BF16),
            p256=p256, g_gv=g_gv[l].reshape(1, wg),
            gq=jnp.tile(g_qk[l, 0], heads * dqk // grp).reshape(1, heads * dqk),
            gk=jnp.tile(g_qk[l, 1], heads * dqk // grp).reshape(1, heads * dqk),
            wp_bd=_block_diag(w_pool[l]).astype(BF16), s_pool=s_pool[l].reshape(1, wa),
            w_s=w_s[l], bsp=jnp.repeat(b_s[l].T, wg // b_s.shape[1], axis=1),
            lamv=lam_vecs[l], g_sub=g_sub[l].reshape(heads, 1, dv),
            keys=sub_keys[l].reshape(nh * two, nk, kd).astype(BF16),
            u_all=u_tab, vt=v_tab[l].T))

    tm = min(256, sp)
    y_p, k_p, v_p, pool_p, _ = _trunk(x_prompt, mods_p, False, None, layer_w, rel_bias,
                                      tm=tm, tq=min(512, sp), tt=min(512, bp * sp), eb=SUBLANES * sub_keys.shape[3])
    y_s, k_s, v_s, pool_s, gv_s = _trunk(x_sample, mods_s, True, (state_pool, cache_k, cache_v), layer_w,
                                         rel_bias, tm=min(256, bs * ss), tq=None, tt=min(512, bs * ss), eb=SUBLANES * sub_keys.shape[3])
    return (y_p, y_s, k_p, v_p, pool_p, k_s, v_s, pool_s, gv_s)
```

```python
import functools
import math

import jax
import jax.numpy as jnp
from jax import lax
from jax.experimental import pallas as pl
from jax.experimental.pallas import tpu as pltpu

F32 = jnp.float32
BF16 = jnp.bfloat16

EPS = 1e-6
ATT_CHUNK = 64
POOL_WINDOWS = (2, 4, 8, 16)
POOL_CARRY = 16
GMLP_CHUNK = 128
REL_BUCKETS = 32
REL_MAX_DIST = 128
PEER_TOPK = 16
NEG_BIG = -1e30

LANES = 128
SUBLANES = 8
VMEM_LIMIT = 56 * 1024 * 1024


def _cparams(sem, flags=None):
    return pltpu.CompilerParams(dimension_semantics=sem, vmem_limit_bytes=VMEM_LIMIT, flags=flags)


def _dot(a, b):
    return lax.dot_general(a, b, (((1,), (0,)), ((), ())), preferred_element_type=F32)


def _idiv_pow2(x, n):
    assert n & (n - 1) == 0, n
    return jnp.right_shift(x, n.bit_length() - 1)


def _dot_nt(a, b):
    return lax.dot_general(a, b, (((1,), (1,)), ((), ())), preferred_element_type=F32)


def _ada_kernel(c_ref, w_ref, b_ref, o_ref):
    c = c_ref[...]
    s = c * jax.nn.sigmoid(c)
    o_ref[...] = _dot(s, w_ref[...]) + b_ref[...]


def _ada_call(c, w, b):
    m, d = c.shape
    n = w.shape[1]
    nb = 1536
    return pl.pallas_call(
        _ada_kernel,
        out_shape=jax.ShapeDtypeStruct((m, n), F32),
        grid=(n // nb,),
        in_specs=[pl.BlockSpec((m, d), lambda j: (0, 0)),
                  pl.BlockSpec((d, nb), lambda j: (0, j)),
                  pl.BlockSpec((1, nb), lambda j: (0, j))],
        out_specs=pl.BlockSpec((m, nb), lambda j: (0, j)),
        compiler_params=_cparams(("arbitrary",)),
        name="ada_mod",
    )(c, w, b.reshape(1, n))


def _rms_rows(x):
    return x * lax.rsqrt(jnp.mean(x * x, axis=-1, keepdims=True) + EPS)


def _group_norm_256(t, p_ref):
    pieces = []
    for s in range(t.shape[1] // 256):
        ts = t[:, s * 256:(s + 1) * 256]
        msq = _dot((ts * ts).astype(BF16), p_ref[...])
        pieces.append(ts * lax.rsqrt(msq + EPS))
    return pieces[0] if len(pieces) == 1 else jnp.concatenate(pieces, axis=-1)


def _k1_kernel(widths, x_ref, sh_ref, sc_ref, g_ref, win_ref, p_ref, gv_ref, gq_ref, gk_ref,
               za_ref, zu_ref, vn_ref, q_ref, k_ref, v_ref):
    h = _rms_rows(x_ref[0]) * g_ref[...] * (1.0 + sc_ref[0]) + sh_ref[0]
    z = _dot(h.astype(BF16), win_ref[...])
    wa, wg, wqk, wv = widths
    o = 0
    za_ref[0] = z[:, o:o + wa]; o += wa
    zu_ref[0] = z[:, o:o + wg]; o += wg
    vn_ref[0] = _group_norm_256(z[:, o:o + wg], p_ref) * gv_ref[...]; o += wg
    q_ref[0] = _group_norm_256(z[:, o:o + wqk], p_ref) * gq_ref[...]; o += wqk
    k_ref[0] = _group_norm_256(z[:, o:o + wqk], p_ref) * gk_ref[...]; o += wqk
    v_ref[0] = z[:, o:o + wv]


def _tok_spec(tm, d):
    return pl.BlockSpec((1, tm, d), lambda b, s: (b, s, 0))


def _mod_spec(per_token, tm, d):
    if per_token:
        return pl.BlockSpec((1, tm, d), lambda b, s: (b, s, 0))
    return pl.BlockSpec((1, 1, d), lambda b, s: (b, 0, 0))


def _full_spec(shape):
    nd = len(shape)
    return pl.BlockSpec(shape, lambda b, s: (0,) * nd)


def _k1_call(x, sh, sc, g, w_in, p256, gv, gq, gk, widths, tm, per_token):
    bsz, seq, d = x.shape
    wa, wg, wqk, wv = widths
    ins = [x, sh, sc, g, w_in, p256, gv, gq, gk]
    specs = [_tok_spec(tm, d), _mod_spec(per_token, tm, d), _mod_spec(per_token, tm, d), _full_spec(g.shape),
             _full_spec(w_in.shape), _full_spec(p256.shape), _full_spec(gv.shape),
             _full_spec(gq.shape), _full_spec(gk.shape)]
    out_w = [wa, wg, wg, wqk, wqk, wv]
    return pl.pallas_call(
        functools.partial(_k1_kernel, widths),
        out_shape=[jax.ShapeDtypeStruct((bsz, seq, w), F32) for w in out_w],
        grid=(bsz, seq // tm),
        in_specs=specs,
        out_specs=[_tok_spec(tm, w) for w in out_w],
        compiler_params=_cparams(("arbitrary", "arbitrary")),
        name="in_proj",
    )(*ins)


def _k2_kernel(pos0, za_ref, hist_ref, zu_ref, vn_ref, wp_ref, sp_ref, ws_ref, bsp_ref,
               ya_ref, yb_ref, carry_ref):
    s = pl.program_id(1)
    tb = za_ref.shape[1]
    w = za_ref.shape[2]
    gw = w // len(POOL_WINDOWS)

    @pl.when(s == 0)
    def _():
        carry_ref[...] = hist_ref[0]

    za = za_ref[0]
    ext = jnp.concatenate([carry_ref[...], za], axis=0)
    carry_ref[...] = za[tb - POOL_CARRY:, :]
    sums = {1: ext}
    d = 1
    while d < max(POOL_WINDOWS):
        sums[2 * d] = sums[d] + pltpu.roll(sums[d], d, axis=0)
        d *= 2
    row = lax.broadcasted_iota(jnp.int32, (tb, w), 0)
    lane_grp = _idiv_pow2(lax.broadcasted_iota(jnp.int32, (tb, w), 1), gw)
    pos1 = (pos0 + s * tb + row + 1).astype(F32)
    means = None
    for gi, win in enumerate(POOL_WINDOWS):
        m = sums[win][POOL_CARRY:, :] / jnp.minimum(float(win), pos1)
        means = m if means is None else jnp.where(lane_grp == gi, m, means)
    dlt = means - za
    ya_ref[0] = _dot(dlt.astype(BF16), wp_ref[...]) * sp_ref[...]

    ll = ws_ref.shape[1]
    r_i = lax.broadcasted_iota(jnp.int32, (ll, ll), 0)
    c_i = lax.broadcasted_iota(jnp.int32, (ll, ll), 1)
    grp = _idiv_pow2(lax.broadcasted_iota(jnp.int32, (ll, w), 1), w // ws_ref.shape[0])
    for c in range(tb // ll):
        vnc = vn_ref[0, c * ll:(c + 1) * ll, :].astype(BF16)
        spv = None
        for gi in range(ws_ref.shape[0]):
            wsg = jnp.where(r_i >= c_i, ws_ref[gi], 0.0).astype(BF16)
            t = _dot(wsg, vnc)
            spv = t if spv is None else jnp.where(grp == gi, t, spv)
        yb_ref[0, c * ll:(c + 1) * ll, :] = zu_ref[0, c * ll:(c + 1) * ll, :] * (spv + bsp_ref[...])


def _k2_call(za, hist16, zu, vn, wp_bd, s_pool, ws, bsp, pos0, tb):
    bsz, seq, w = za.shape
    return pl.pallas_call(
        functools.partial(_k2_kernel, pos0),
        out_shape=[jax.ShapeDtypeStruct((bsz, seq, w), F32)] * 2,
        grid=(bsz, seq // tb),
        in_specs=[_tok_spec(tb, w),
                  pl.BlockSpec((1, POOL_CARRY, w), lambda b, s: (b, 0, 0)),
                  _tok_spec(tb, w), _tok_spec(tb, w),
                  _full_spec(wp_bd.shape), _full_spec(s_pool.shape), _full_spec(ws.shape),
                  _full_spec(bsp.shape)],
        out_specs=[_tok_spec(tb, w)] * 2,
        scratch_shapes=[pltpu.VMEM((POOL_CARRY, w), F32)],
        compiler_params=_cparams(("arbitrary", "arbitrary")),
        name="pool_gmlp",
    )(za, hist16, zu, vn, wp_bd, s_pool, ws, bsp)


def _lambda(lamv_ref, lam_init):
    lv = lamv_ref[...]
    a = jnp.sum(lv[0:1] * lv[1:2], axis=-1, keepdims=True)
    b = jnp.sum(lv[2:3] * lv[3:4], axis=-1, keepdims=True)
    return jnp.exp(a) - jnp.exp(b) + lam_init


def _split_maps(q):
    lane = lax.broadcasted_iota(jnp.int32, q.shape, 1)
    half = q.shape[1] // 2
    return jnp.where(lane < half, q, 0.0), jnp.where(lane >= half, q, 0.0)


def _online_update(state, s, vblk):
    m, l, acc = state
    m_new = jnp.maximum(m, jnp.max(s, axis=-1, keepdims=True))
    alpha = jnp.exp(m - m_new)
    p = jnp.exp(s - m_new)
    l_new = alpha * l + jnp.sum(p, axis=-1, keepdims=True)
    acc_new = alpha * acc + _dot(p, vblk)
    return m_new, l_new, acc_new


def _sub_norm(o0, o1, lam, lam_init, gsub):
    o = o0 - lam * o1
    return _rms_rows(o) * gsub * (1.0 - lam_init)


def _attn_prompt_kernel(lam_init, scale, q_ref, k_ref, v_ref, bias_ref, lamv_ref, gsub_ref, o_ref):
    qb = pl.program_id(2)
    tq = q_ref.shape[1]
    dv = v_ref.shape[2]
    q0, q1 = _split_maps(q_ref[0] * scale)

    def init():
        return (jnp.full((tq, 1), NEG_BIG, F32), jnp.zeros((tq, 1), F32), jnp.zeros((tq, dv), F32))

    def block(kb, bias):
        kblk = k_ref[0, pl.ds(pl.multiple_of(kb * tq, tq), tq), :].astype(BF16)
        vblk = v_ref[0, pl.ds(pl.multiple_of(kb * tq, tq), tq), :].astype(BF16)
        return _dot_nt(q0, kblk) + bias, _dot_nt(q1, kblk) + bias, vblk

    def body(kb, carry):
        st0, st1 = carry
        bias = bias_ref[0, jnp.minimum(qb - kb, 2)]
        s0, s1, vblk = block(kb, bias)
        return _online_update(st0, s0, vblk), _online_update(st1, s1, vblk)

    st0, st1 = lax.fori_loop(0, qb, body, (init(), init()))
    s0, s1, vblk = block(qb, bias_ref[0, 0])
    qi = _idiv_pow2(lax.broadcasted_iota(jnp.int32, (tq, tq), 0), ATT_CHUNK)
    ki = _idiv_pow2(lax.broadcasted_iota(jnp.int32, (tq, tq), 1), ATT_CHUNK)
    vis = ki <= qi
    st0 = _online_update(st0, jnp.where(vis, s0, NEG_BIG), vblk)
    st1 = _online_update(st1, jnp.where(vis, s1, NEG_BIG), vblk)
    lam = _lambda(lamv_ref, lam_init)
    o_ref[0] = _sub_norm(st0[2] / st0[1], st1[2] / st1[1], lam, lam_init, gsub_ref[0])


def _attn_prompt_call(qn, kn, v, bias_tiles, lamv, gsub, lam_init, heads, tq):
    bsz, seq, _ = qn.shape
    dh = qn.shape[2] // heads
    dv = v.shape[2] // heads
    scale = float((dh // 2) ** -0.5)
    return pl.pallas_call(
        functools.partial(_attn_prompt_kernel, lam_init, scale),
        out_shape=jax.ShapeDtypeStruct((bsz, seq, heads * dv), F32),
        grid=(bsz, heads, seq // tq),
        in_specs=[pl.BlockSpec((1, tq, dh), lambda b, h, i: (b, i, h)),
                  pl.BlockSpec((1, seq, dh), lambda b, h, i: (b, 0, h)),
                  pl.BlockSpec((1, seq, dv), lambda b, h, i: (b, 0, h)),
                  pl.BlockSpec((1, 3, tq, tq), lambda b, h, i: (h, 0, 0, 0)),
                  pl.BlockSpec(lamv.shape, lambda b, h, i: (0, 0)),
                  pl.BlockSpec((1, 1, dv), lambda b, h, i: (h, 0, 0))],
        out_specs=pl.BlockSpec((1, tq, dv), lambda b, h, i: (b, i, h)),
        compiler_params=_cparams(("arbitrary", "arbitrary", "arbitrary")),
        name="attn_prompt",
    )(qn, kn, v, bias_tiles, lamv, gsub)


def _attn_sample_kernel(lam_init, scale, heads, q_ref, kc_ref, vc_ref, kn_ref, vn_ref,
                        bc_ref, bn_ref, lamv_ref, gsub_ref, o_ref):
    t = q_ref.shape[1]
    dh = q_ref.shape[2] // heads
    dv = vc_ref.shape[3]
    lam = _lambda(lamv_ref, lam_init)
    pad = bn_ref.shape[2] - t
    for h in range(heads):
        q0, q1 = _split_maps(q_ref[0, :, h * dh:(h + 1) * dh] * scale)
        kc = kc_ref[0, :, h, :].astype(BF16)
        vc = vc_ref[0, :, h, :].astype(BF16)
        kn = jnp.concatenate([kn_ref[0, :, h * dh:(h + 1) * dh], jnp.zeros((pad, dh), F32)], axis=0).astype(BF16)
        vn = jnp.concatenate([vn_ref[0, :, h * dv:(h + 1) * dv], jnp.zeros((pad, dv), F32)], axis=0).astype(BF16)
        outs = []
        for qm in (q0, q1):
            sc = _dot_nt(qm, kc) + bc_ref[h]
            sn = _dot_nt(qm, kn) + bn_ref[h]
            m = jnp.maximum(jnp.max(sc, axis=-1, keepdims=True), jnp.max(sn, axis=-1, keepdims=True))
            pc = jnp.exp(sc - m)
            pn = jnp.exp(sn - m)
            l = jnp.sum(pc, axis=-1, keepdims=True) + jnp.sum(pn, axis=-1, keepdims=True)
            outs.append((_dot(pc, vc) + _dot(pn, vn)) / l)
        o_ref[0, :, h * dv:(h + 1) * dv] = _sub_norm(outs[0], outs[1], lam, lam_init, gsub_ref[h])


def _attn_sample_call(qn, cache_k, cache_v, layer, kn, vn, bias_c, bias_n, lamv, gsub, lam_init, heads):
    bsz, t, wq = qn.shape
    wv = heads * cache_v.shape[4]
    dh = wq // heads
    scale = float((dh // 2) ** -0.5)
    row = lambda shape: pl.BlockSpec((1,) + shape[1:], lambda b: (b, 0, 0))
    full = lambda shape: pl.BlockSpec(shape, lambda b: (0,) * len(shape))
    cache = lambda shape: pl.BlockSpec((None, 1) + shape[2:], lambda b: (layer, b, 0, 0, 0))
    return pl.pallas_call(
        functools.partial(_attn_sample_kernel, lam_init, scale, heads),
        out_shape=jax.ShapeDtypeStruct((bsz, t, wv), F32),
        grid=(bsz,),
        in_specs=[row(qn.shape), cache(cache_k.shape), cache(cache_v.shape), row(kn.shape), row(vn.shape),
                  full(bias_c.shape), full(bias_n.shape), full(lamv.shape), full(gsub.shape)],
        out_specs=row((bsz, t, wv)),
        compiler_params=_cparams(("arbitrary",)),
        name="attn_sample",
    )(qn, cache_k, cache_v, kn, vn, bias_c, bias_n, lamv, gsub)


def _k3_kernel(x_ref, ya_ref, yb_ref, o_ref, gt_ref, sh_ref, sc_ref, g_ref, wout_ref, wpq_ref, keys_ref,
               x1_ref, st_ref, h2t_ref):
    wa = ya_ref.shape[2]
    wb = yb_ref.shape[2]
    y = (_dot(ya_ref[0].astype(BF16), wout_ref[0:wa, :])
         + _dot(yb_ref[0].astype(BF16), wout_ref[wa:wa + wb, :])
         + _dot(o_ref[0].astype(BF16), wout_ref[wa + wb:, :]))
    x1 = x_ref[0] + gt_ref[0] * y
    x1_ref[0] = x1
    h2 = _rms_rows(x1) * g_ref[...] * (1.0 + sc_ref[0]) + sh_ref[0]
    h2t_ref[...] = h2.T.astype(BF16)
    qp = _dot(h2.astype(BF16), wpq_ref[...])
    nk = keys_ref.shape[1]
    kd = keys_ref.shape[2]
    for r in range(keys_ref.shape[0]):
        st_ref[r * nk:(r + 1) * nk, :] = _dot_nt(keys_ref[r], qp[:, r * kd:(r + 1) * kd].astype(BF16))


def _k3_call(x, ya, yb, o, gt, sh, sc, g, w_out, w_pq, keys, tm, per_token):
    bsz, seq, d = x.shape
    n = bsz * seq
    ns = seq // tm
    nrow = keys.shape[0] * keys.shape[1]
    flat = lambda b, s: (0, b * ns + s)
    return pl.pallas_call(
        _k3_kernel,
        out_shape=[jax.ShapeDtypeStruct((bsz, seq, d), F32),
                   jax.ShapeDtypeStruct((nrow, n), F32),
                   jax.ShapeDtypeStruct((d, n), BF16)],
        grid=(bsz, ns),
        in_specs=[_tok_spec(tm, d), _tok_spec(tm, ya.shape[2]), _tok_spec(tm, yb.shape[2]),
                  _tok_spec(tm, o.shape[2]),
                  _mod_spec(per_token, tm, d), _mod_spec(per_token, tm, d), _mod_spec(per_token, tm, d),
                  _full_spec(g.shape), _full_spec(w_out.shape), _full_spec(w_pq.shape),
                  _full_spec(keys.shape)],
        out_specs=[_tok_spec(tm, d), pl.BlockSpec((nrow, tm), flat), pl.BlockSpec((d, tm), flat)],
        compiler_params=_cparams(("arbitrary", "arbitrary")),
        name="out_proj_peer_scores",
    )(x, ya, yb, o, gt, sh, sc, g, w_out, w_pq, keys)


def _sort_network(n):
    def merge(lo, hi, r):
        step = r * 2
        if step < hi - lo:
            yield from merge(lo, hi, step)
            yield from merge(lo + r, hi, step)
            yield from ((i, i + r) for i in range(lo + r, hi - r, step))
        else:
            yield (lo, lo + r)

    def sort(lo, hi):
        if hi - lo >= 1:
            mid = lo + (hi - lo) // 2
            yield from sort(lo, mid)
            yield from sort(mid + 1, hi)
            yield from merge(lo, hi, 1)

    return list(sort(0, n - 1))


def _compare_exchange(vs, i, j):
    a, b = vs[i], vs[j]
    if b is None:
        return
    if a is None:
        vs[i], vs[j] = b, None
        return
    vs[i], vs[j] = jnp.maximum(a, b), jnp.minimum(a, b)


def _sort_desc(vs):
    vs = list(vs)
    for i, j in _sort_network(len(vs)):
        _compare_exchange(vs, i, j)
    return vs


def _top_across_sublanes(vs):
    n = len(vs)
    for shift in (4, 2, 1):
        mixed = []
        for k in range(n):
            a, b = vs[k], vs[n - 1 - k]
            if b is None:
                mixed.append(a)
            else:
                b = pltpu.roll(b, shift, axis=0)
                mixed.append(b if a is None else jnp.maximum(a, b))
        vs = mixed
        d = n // 2
        while d >= 1:
            for k in range(n):
                if k & d == 0:
                    _compare_exchange(vs, k, k + d)
            d //= 2
    return vs


def _all_sublanes(x, op):
    for shift in (4, 2, 1):
        x = op(x, pltpu.roll(x, shift, axis=0))
    return x


def _next_below(vals, cut, k):
    cnt = None
    below = None
    for v in vals:
        c = jnp.where(v >= cut, 1.0, 0.0)
        b = jnp.where(v < cut, v, -jnp.inf)
        cnt = c if cnt is None else cnt + c
        below = b if below is None else jnp.maximum(below, b)
    cnt = _all_sublanes(cnt, jnp.add)
    below = _all_sublanes(below, jnp.maximum)
    return jnp.where(cnt > float(k), cut, below)


PEER_TOP_ROWS = 24


def _k4_kernel(st_ref, stats_ref, top_ref):
    nlist = top_ref.shape[0]
    nk = st_ref.shape[0] // nlist
    k = PEER_TOPK
    assert nk // SUBLANES == k and k == 2 * SUBLANES

    def chunk(c, _):
        cs = pl.ds(pl.multiple_of(c * LANES, LANES), LANES)

        def lists(hh, _):
            for r in (2 * hh, 2 * hh + 1):
                base = pl.multiple_of(r * nk, nk)
                vals = [st_ref[pl.ds(base + g * SUBLANES, SUBLANES), cs] for g in range(nk // SUBLANES)]
                top = _top_across_sublanes(_sort_desc(vals))
                for i in range(k):
                    top_ref[r, i:i + 1, :] = top[i][0:1]
                top_ref[r, k:k + 1, :] = _next_below(vals, top[k - 1], k)[0:1]
            return 0

        lax.fori_loop(0, nlist // 2, lists, 0)

        def one_head(h):
            a_hi = top_ref[2 * h, SUBLANES:k]
            b_lo, b_hi = top_ref[2 * h + 1, 0:SUBLANES], top_ref[2 * h + 1, SUBLANES:k]
            a0, b0 = top_ref[2 * h, 0:1], top_ref[2 * h + 1, 0:1]
            cands = [a0 + b_lo, a0 + b_hi]
            cands += [top_ref[2 * h, p:p + 1] + b_lo for p in range(1, SUBLANES)]
            cands.append(a_hi + b0)
            tv = _top_across_sublanes(_sort_desc(cands + [None] * (k - len(cands))))
            v16 = tv[k - 1]
            v17 = _next_below(cands, v16, k)
            v17 = jnp.maximum(v17, jnp.maximum(top_ref[2 * h, k:k + 1] + b0, a0 + top_ref[2 * h + 1, k:k + 1]))
            z = jnp.ones_like(v16)
            for r in range(1, k):
                z = z + jnp.exp(tv[r] - tv[0])
            zero = jnp.zeros_like(a0)
            stats_ref[h, :, cs] = jnp.concatenate(
                [(0.5 * (v16 + v17))[0:1], a0, b0, (1.0 / z)[0:1], zero, zero, zero, zero], axis=0)

        def heads(hp, _):
            one_head(2 * hp)
            one_head(2 * hp + 1)
            return 0

        lax.fori_loop(0, nlist // 4, heads, 0)
        return 0

    lax.fori_loop(0, st_ref.shape[1] // LANES, chunk, 0)


def _k4_call(st, nlist, tt):
    nrow, n = st.shape
    nh = nlist // 2
    return pl.pallas_call(
        _k4_kernel,
        out_shape=jax.ShapeDtypeStruct((nh, SUBLANES, n), F32),
        grid=(n // tt,),
        in_specs=[pl.BlockSpec((nrow, tt), lambda i: (0, i))],
        out_specs=pl.BlockSpec((nh, SUBLANES, tt), lambda i: (0, 0, i)),
        scratch_shapes=[pltpu.VMEM((nlist, PEER_TOP_ROWS, LANES), F32)],
        compiler_params=_cparams(("arbitrary",)),
        name="peer_topk_stats",
    )(st)


PEER_ROWS = 32


PEER_PARTS = 2


def _k5_kernel(nheads, nk, ne, ht_ref, st_ref, stats_ref, u_ref, vt_ref, x_ref, gt_ref, out_ref,
               c1_ref, e1_ref, e2_ref, at0_ref, at1_ref, g0_ref, g1_ref, acc_ref):
    s = pl.program_id(0)
    sb = jnp.maximum(s - 1, 0)
    sc = jnp.maximum(s - 2, 0)
    e = sb % ne
    eb = u_ref.shape[0]
    tt = ht_ref.shape[1]
    d = vt_ref.shape[0]
    ni = eb // nk
    assert ni == SUBLANES, (eb, nk)
    grp = nk // SUBLANES
    sqrt_half = math.sqrt(0.5)

    @pl.when(s == 0)
    def _():
        for r in (at0_ref, at1_ref, g0_ref, g1_ref):
            r[...] = jnp.zeros_like(r)

    @pl.when(sc % ne == 0)
    def _():
        acc_ref[...] = jnp.zeros_like(acc_ref)

    @pl.when(e == 0)
    def _():
        for h in range(nheads):
            m1 = stats_ref[h, 1:2, :]
            thr = stats_ref[h, 0:1, :]
            for g in range(grp):
                s1 = st_ref[2 * h * nk + g * SUBLANES:2 * h * nk + (g + 1) * SUBLANES, :]
                c1_ref[h * grp + g] = thr - s1
                e1_ref[h * grp + g] = jnp.exp(s1 - m1)
            s2 = st_ref[(2 * h + 1) * nk:(2 * h + 2) * nk, :]
            e2_ref[h * nk:(h + 1) * nk, :] = jnp.exp(s2 - stats_ref[h, 2:3, :]) * (0.5 * stats_ref[h, 3:4, :])

    assert PEER_PARTS % 2 == 0 and (tt // LANES) % PEER_PARTS == 0
    msplit = PEER_PARTS // 2
    mh_rows = eb // msplit
    dh_rows = d // msplit
    nh_cols = tt // 2
    chunks = tt // LANES // PEER_PARTS

    def part(at_w, at_r, g_w, g_r, k, _):
        mh = k // 2
        nc = pl.ds(pl.multiple_of((k % 2) * nh_cols, nh_cols), nh_cols)
        er = pl.ds(pl.multiple_of(mh * mh_rows, mh_rows), mh_rows)
        at_w[er, nc] = _dot(u_ref[er, :], ht_ref[:, nc])
        for c in range(chunks):
            cs = pl.ds(pl.multiple_of((k * chunks + c) * LANES, LANES), LANES)
            for j0 in range(0, nk, PEER_ROWS):
                w = [jnp.zeros((PEER_ROWS, LANES), F32) for _ in range(ni)]
                for h in range(nheads):
                    s2 = st_ref[(2 * h + 1) * nk + j0:(2 * h + 1) * nk + j0 + PEER_ROWS, cs]
                    e2 = e2_ref[h * nk + j0:h * nk + j0 + PEER_ROWS, cs]
                    for ii in range(ni):
                        sel = s2 >= c1_ref[h * grp + e, ii:ii + 1, cs]
                        w[ii] = w[ii] + e1_ref[h * grp + e, ii:ii + 1, cs] * jnp.where(sel, e2, 0.0)
                for ii in range(ni):
                    rows = slice(ii * nk + j0, ii * nk + j0 + PEER_ROWS)
                    a = at_r[rows, cs]
                    g_w[rows, cs] = ((w[ii] * a) * (1.0 + lax.erf(a * sqrt_half))).astype(BF16)
        dr = pl.ds(pl.multiple_of(mh * dh_rows, dh_rows), dh_rows)
        acc_ref[dr, nc] += _dot(vt_ref[dr, :], g_r[:, nc])
        return 0

    @pl.when(s % 2 == 0)
    def _():
        lax.fori_loop(0, PEER_PARTS, functools.partial(part, at0_ref, at1_ref, g1_ref, g0_ref), 0)

    @pl.when(s % 2 == 1)
    def _():
        lax.fori_loop(0, PEER_PARTS, functools.partial(part, at1_ref, at0_ref, g0_ref, g1_ref), 0)

    @pl.when(jnp.logical_and(s >= 2, sc % ne == ne - 1))
    def _():
        out_ref[...] = x_ref[...] + gt_ref[0] * acc_ref[...].T


def _k5_call(h2t, st, stats, u_all, layer, vt, x, gt, per_token, nheads, nk, tt, eb):
    d, n = h2t.shape
    assert vt.shape[1:] == (d, eb), vt.shape
    seq = x.shape[1]
    assert seq % tt == 0
    tiles_per_row = seq // tt
    ne = u_all.shape[1] // eb
    nt = n // tt
    tile_a = lambda s: jnp.minimum(s // ne, nt - 1)
    tile_b = lambda s: jnp.minimum(jnp.maximum(s - 1, 0) // ne, nt - 1)
    tile_c = lambda s: jnp.minimum(jnp.maximum(s - 2, 0) // ne, nt - 1)
    return pl.pallas_call(
        functools.partial(_k5_kernel, nheads, nk, ne),
        out_shape=jax.ShapeDtypeStruct((n, d), F32),
        grid=(nt * ne + 2,),
        in_specs=[pl.BlockSpec((d, tt), lambda s: (0, tile_a(s))),
                  pl.BlockSpec((st.shape[0], tt), lambda s: (0, tile_b(s))),
                  pl.BlockSpec((nheads, SUBLANES, tt), lambda s: (0, 0, tile_b(s))),
                  pl.BlockSpec((None, eb, d), lambda s: (layer, s % ne, 0)),
                  pl.BlockSpec((None, d, eb), lambda s: (jnp.maximum(s - 2, 0) % ne, 0, 0)),
                  pl.BlockSpec((tt, d), lambda s: (tile_c(s), 0)),
                  (pl.BlockSpec((1, tt, d), lambda s: (0, tile_c(s), 0)) if per_token else
                   pl.BlockSpec((1, 1, d), lambda s: (tile_c(s) // tiles_per_row, 0, 0)))],
        out_specs=pl.BlockSpec((tt, d), lambda s: (tile_c(s), 0)),
        scratch_shapes=[pltpu.VMEM((nheads * nk // SUBLANES, SUBLANES, tt), F32),
                        pltpu.VMEM((nheads * nk // SUBLANES, SUBLANES, tt), F32),
                        pltpu.VMEM((nheads * nk, tt), F32),
                        pltpu.VMEM((eb, tt), F32), pltpu.VMEM((eb, tt), F32),
                        pltpu.VMEM((eb, tt), BF16), pltpu.VMEM((eb, tt), BF16),
                        pltpu.VMEM((d, tt), F32)],
        compiler_params=_cparams(("arbitrary",)),
        name="peer_dense",
    )(h2t, st, stats, u_all, vt, x.reshape(n, d), gt)


def _t5_bucket(rel):
    nb = REL_BUCKETS // 2
    max_exact = nb // 2
    side = jnp.where(rel > 0, nb, 0)
    n = jnp.abs(rel)
    nf = jnp.maximum(n, 1).astype(jnp.float32)
    large = max_exact + (jnp.log(nf / max_exact) / math.log(REL_MAX_DIST / max_exact)
                         * (nb - max_exact)).astype(jnp.int32)
    large = jnp.minimum(large, nb - 1)
    return side + jnp.where(n < max_exact, n, large)


def _bias_table(rel_bias, q_pos, k_pos):
    bucket = _t5_bucket(k_pos[None, :] - q_pos[:, None])
    rb = rel_bias.astype(F32)
    out = jnp.zeros((rb.shape[1],) + bucket.shape, F32)
    for b in range(rb.shape[0]):
        out = jnp.where(bucket[None] == b, rb[b][:, None, None], out)
    return out


def _block_diag(w):
    g, a, b = w.shape
    eye = jnp.eye(g, dtype=w.dtype)
    return (eye[:, None, :, None] * w[:, :, None, :]).reshape(g * a, g * b)


def _trunk(x, c_mod, per_token, sample, layer_w, rel_bias, tm, tq, tt, eb):
    bsz, seq, d = x.shape
    depth = len(layer_w)
    k_out, v_out, pool_out, gv_out = [], [], [], []
    tok = (lambda t: t.reshape(1, bsz * seq, t.shape[-1])) if per_token else (lambda t: t)
    seqv = lambda t: t.reshape(bsz, seq, t.shape[-1])
    xt = tok(x)
    if sample is None:
        pos = jnp.arange(tq)
        tiles = jnp.stack([_bias_table(rel_bias, pos, pos - dlt * tq) for dlt in range(3)], axis=1)
    else:
        past = sample[1].shape[2]
        q_pos = past + jnp.arange(seq)
        bias_c = _bias_table(rel_bias, q_pos, jnp.arange(past))
        bias_n = jnp.pad(_bias_table(rel_bias, q_pos, q_pos), ((0, 0), (0, 0), (0, LANES - seq)),
                         constant_values=NEG_BIG)
    for l in range(depth):
        w = layer_w[l]
        sh1, sc1, gt1, sh2, sc2, gt2 = c_mod[l]
        heads = w["heads"]
        widths = w["widths"]
        outs = _k1_call(xt, sh1, sc1, w["g1"], w["w_in"], w["p256"], w["g_gv"],
                        w["gq"], w["gk"], widths, tm, per_token)
        za, zu, vn, qn, kn, v = [seqv(t) for t in outs]
        if sample is None:
            hist = jnp.zeros((bsz, POOL_CARRY, za.shape[2]), F32)
            ya, yb = _k2_call(za, hist, zu, vn, w["wp_bd"], w["s_pool"], w["w_s"], w["bsp"], 0, 2 * GMLP_CHUNK)
            pool_out.append(za[:, seq - (POOL_CARRY - 1):])
        else:
            state_pool, cache_k, cache_v = sample
            hist = jnp.pad(state_pool[l], ((0, 0), (1, 0), (0, 0)))
            padr = ((0, 0), (0, GMLP_CHUNK - seq), (0, 0))
            ya, yb = _k2_call(jnp.pad(za, padr), hist, jnp.pad(zu, padr), jnp.pad(vn, padr), w["wp_bd"],
                              w["s_pool"], w["w_s"], w["bsp"], cache_k.shape[2], GMLP_CHUNK)
            ya, yb = ya[:, :seq], yb[:, :seq]
            pool_out.append(jnp.concatenate([state_pool[l], za], axis=1)[:, -(POOL_CARRY - 1):])
            gv_out.append(vn)
        lam_init = 0.8 - 0.6 * math.exp(-0.3 * l)
        if sample is None:
            o = _attn_prompt_call(qn, kn, v, tiles, w["lamv"], w["g_sub"], lam_init, heads, tq)
        else:
            o = _attn_sample_call(qn, cache_k, cache_v, l, kn, v, bias_c, bias_n, w["lamv"], w["g_sub"],
                                  lam_init, heads)
        k_out.append(kn.reshape(bsz, seq, heads, -1))
        v_out.append(v.reshape(bsz, seq, heads, -1))
        xt, st, h2t = _k3_call(xt, tok(ya), tok(yb), tok(o), gt1, sh2, sc2, w["g2"], w["w_out"], w["w_pq"],
                               w["keys"], tm, per_token)
        nlist = w["keys"].shape[0]
        nk = w["keys"].shape[1]
        stats = _k4_call(st, nlist, tt)
        xt = _k5_call(h2t, st, stats, w["u_all"], l, w["vt"], xt, gt2, per_token, nlist // 2, nk, tt,
                      eb).reshape(xt.shape)
    y = xt.reshape(bsz, seq, d)
    return y, jnp.stack(k_out), jnp.stack(v_out), jnp.stack(pool_out), (jnp.stack(gv_out) if gv_out else None)


def kernel(x_prompt, x_sample, c_prompt, c_sample, cache_k, cache_v, state_pool, rel_bias, w_ada, b_ada,
           g_norm, w_in, w_out, w_pool, s_pool, g_gv, w_s, b_s, g_qk, lam_vecs, g_sub, w_pq, sub_keys,
           u_tab, v_tab):
    depth = w_in.shape[0]
    d = x_prompt.shape[2]
    bp, sp = x_prompt.shape[:2]
    bs, ss = x_sample.shape[:2]
    heads = cache_k.shape[3]
    dqk = cache_k.shape[4]
    dv = cache_v.shape[4]
    wa = w_pool.shape[1] * w_pool.shape[2]
    wg = g_gv.shape[1]
    widths = (wa, wg, heads * dqk, heads * dv)
    grp = dqk // 2

    p256 = _block_diag(jnp.full((256 // grp, grp, grp), 1.0 / grp, F32)).astype(BF16)
    c_all = jnp.concatenate([c_prompt, c_sample], axis=0)
    layer_w, mods_p, mods_s = [], [], []
    for l in range(depth):
        mod = _ada_call(c_all, w_ada[l], b_ada[l])
        six = jnp.split(mod, 6, axis=-1)
        mods_p.append([m[:bp, None, :] for m in six])
        mods_s.append([jnp.broadcast_to(m[bp:, None, :], (bs, ss, d)).reshape(1, bs * ss, d) for m in six])
        nh, two, nk, kd = sub_keys.shape[1:]
        layer_w.append(dict(
            heads=heads, widths=widths,
            g1=g_norm[l, 0].reshape(1, d), g2=g_norm[l, 1].reshape(1, d),
            w_in=w_in[l].astype(BF16), w_out=w_out[l].astype(BF16), w_pq=w_pq[l].astype(BF16),
            p256=p256, g_gv=g_gv[l].reshape(1, wg),
            gq=jnp.tile(g_qk[l, 0], heads * dqk // grp).reshape(1, heads * dqk),
            gk=jnp.tile(g_qk[l, 1], heads * dqk // grp).reshape(1, heads * dqk),
            wp_bd=_block_diag(w_pool[l]).astype(BF16), s_pool=s_pool[l].reshape(1, wa),
            w_s=w_s[l], bsp=jnp.repeat(b_s[l].T, wg // b_s.shape[1], axis=1),
            lamv=lam_vecs[l], g_sub=g_sub[l].reshape(heads, 1, dv),
            keys=sub_keys[l].reshape(nh * two, nk, kd).astype(BF16),
            u_all=u_tab, vt=jnp.transpose(v_tab[l].reshape(-1, SUBLANES * nk, d), (0, 2, 1))))

    tm = min(256, sp)
    y_p, k_p, v_p, pool_p, _ = _trunk(x_prompt, mods_p, False, None, layer_w, rel_bias,
                                      tm=tm, tq=min(512, sp), tt=min(512, bp * sp), eb=SUBLANES * sub_keys.shape[3])
    y_s, k_s, v_s, pool_s, gv_s = _trunk(x_sample, mods_s, True, (state_pool, cache_k, cache_v), layer_w,
                                         rel_bias, tm=min(256, bs * ss), tq=None, tt=min(512, bs * ss), eb=SUBLANES * sub_keys.shape[3])
    return (y_p, y_s, k_p, v_p, pool_p, k_s, v_s, pool_s, gv_s)
```

```python
import functools
import math

import jax
import jax.numpy as jnp
from jax import lax
from jax.experimental import pallas as pl
from jax.experimental.pallas import tpu as pltpu

F32 = jnp.float32
BF16 = jnp.bfloat16

EPS = 1e-6
ATT_CHUNK = 64
POOL_WINDOWS = (2, 4, 8, 16)
POOL_CARRY = 16
GMLP_CHUNK = 128
REL_BUCKETS = 32
REL_MAX_DIST = 128
PEER_TOPK = 16
NEG_BIG = -1e30

LANES = 128
SUBLANES = 8
VMEM_LIMIT = 56 * 1024 * 1024


def _cparams(sem, flags=None):
    return pltpu.CompilerParams(dimension_semantics=sem, vmem_limit_bytes=VMEM_LIMIT, flags=flags)


def _dot(a, b):
    return lax.dot_general(a, b, (((1,), (0,)), ((), ())), preferred_element_type=F32)


def _idiv_pow2(x, n):
    assert n & (n - 1) == 0, n
    return jnp.right_shift(x, n.bit_length() - 1)


def _dot_nt(a, b):
    return lax.dot_general(a, b, (((1,), (1,)), ((), ())), preferred_element_type=F32)


def _ada_kernel(c_ref, w_ref, b_ref, o_ref):
    c = c_ref[...]
    s = c * jax.nn.sigmoid(c)
    o_ref[...] = _dot(s, w_ref[...]) + b_ref[...]


def _ada_call(c, w, b):
    m, d = c.shape
    n = w.shape[1]
    nb = 1536
    return pl.pallas_call(
        _ada_kernel,
        out_shape=jax.ShapeDtypeStruct((m, n), F32),
        grid=(n // nb,),
        in_specs=[pl.BlockSpec((m, d), lambda j: (0, 0)),
                  pl.BlockSpec((d, nb), lambda j: (0, j)),
                  pl.BlockSpec((1, nb), lambda j: (0, j))],
        out_specs=pl.BlockSpec((m, nb), lambda j: (0, j)),
        compiler_params=_cparams(("arbitrary",)),
        name="ada_mod",
    )(c, w, b.reshape(1, n))


def _rms_rows(x):
    return x * lax.rsqrt(jnp.mean(x * x, axis=-1, keepdims=True) + EPS)


def _group_norm_256(t, p_ref):
    pieces = []
    for s in range(t.shape[1] // 256):
        ts = t[:, s * 256:(s + 1) * 256]
        msq = _dot((ts * ts).astype(BF16), p_ref[...])
        pieces.append(ts * lax.rsqrt(msq + EPS))
    return pieces[0] if len(pieces) == 1 else jnp.concatenate(pieces, axis=-1)


def _k1_kernel(widths, x_ref, sh_ref, sc_ref, g_ref, win_ref, p_ref, gv_ref, gq_ref, gk_ref,
               za_ref, zu_ref, vn_ref, q_ref, k_ref, v_ref):
    h = _rms_rows(x_ref[0]) * g_ref[...] * (1.0 + sc_ref[0]) + sh_ref[0]
    z = _dot(h.astype(BF16), win_ref[...])
    wa, wg, wqk, wv = widths
    o = 0
    za_ref[0] = z[:, o:o + wa]; o += wa
    zu_ref[0] = z[:, o:o + wg]; o += wg
    vn_ref[0] = _group_norm_256(z[:, o:o + wg], p_ref) * gv_ref[...]; o += wg
    q_ref[0] = _group_norm_256(z[:, o:o + wqk], p_ref) * gq_ref[...]; o += wqk
    k_ref[0] = _group_norm_256(z[:, o:o + wqk], p_ref) * gk_ref[...]; o += wqk
    v_ref[0] = z[:, o:o + wv]


def _tok_spec(tm, d):
    return pl.BlockSpec((1, tm, d), lambda b, s: (b, s, 0))


def _mod_spec(per_token, tm, d):
    if per_token:
        return pl.BlockSpec((1, tm, d), lambda b, s: (b, s, 0))
    return pl.BlockSpec((1, 1, d), lambda b, s: (b, 0, 0))


def _full_spec(shape):
    nd = len(shape)
    return pl.BlockSpec(shape, lambda b, s: (0,) * nd)


def _k1_call(x, sh, sc, g, w_in, p256, gv, gq, gk, widths, tm, per_token):
    bsz, seq, d = x.shape
    wa, wg, wqk, wv = widths
    ins = [x, sh, sc, g, w_in, p256, gv, gq, gk]
    specs = [_tok_spec(tm, d), _mod_spec(per_token, tm, d), _mod_spec(per_token, tm, d), _full_spec(g.shape),
             _full_spec(w_in.shape), _full_spec(p256.shape), _full_spec(gv.shape),
             _full_spec(gq.shape), _full_spec(gk.shape)]
    out_w = [wa, wg, wg, wqk, wqk, wv]
    return pl.pallas_call(
        functools.partial(_k1_kernel, widths),
        out_shape=[jax.ShapeDtypeStruct((bsz, seq, w), F32) for w in out_w],
        grid=(bsz, seq // tm),
        in_specs=specs,
        out_specs=[_tok_spec(tm, w) for w in out_w],
        compiler_params=_cparams(("arbitrary", "arbitrary")),
        name="in_proj",
    )(*ins)


def _k2_kernel(pos0, za_ref, hist_ref, zu_ref, vn_ref, wp_ref, sp_ref, ws_ref, bsp_ref,
               ya_ref, yb_ref, carry_ref):
    s = pl.program_id(1)
    tb = za_ref.shape[1]
    w = za_ref.shape[2]
    gw = w // len(POOL_WINDOWS)

    @pl.when(s == 0)
    def _():
        carry_ref[...] = hist_ref[0]

    za = za_ref[0]
    ext = jnp.concatenate([carry_ref[...], za], axis=0)
    carry_ref[...] = za[tb - POOL_CARRY:, :]
    sums = {1: ext}
    d = 1
    while d < max(POOL_WINDOWS):
        sums[2 * d] = sums[d] + pltpu.roll(sums[d], d, axis=0)
        d *= 2
    row = lax.broadcasted_iota(jnp.int32, (tb, w), 0)
    lane_grp = _idiv_pow2(lax.broadcasted_iota(jnp.int32, (tb, w), 1), gw)
    pos1 = (pos0 + s * tb + row + 1).astype(F32)
    means = None
    for gi, win in enumerate(POOL_WINDOWS):
        m = sums[win][POOL_CARRY:, :] / jnp.minimum(float(win), pos1)
        means = m if means is None else jnp.where(lane_grp == gi, m, means)
    dlt = means - za
    ya_ref[0] = _dot(dlt.astype(BF16), wp_ref[...]) * sp_ref[...]

    ll = ws_ref.shape[1]
    r_i = lax.broadcasted_iota(jnp.int32, (ll, ll), 0)
    c_i = lax.broadcasted_iota(jnp.int32, (ll, ll), 1)
    grp = _idiv_pow2(lax.broadcasted_iota(jnp.int32, (ll, w), 1), w // ws_ref.shape[0])
    for c in range(tb // ll):
        vnc = vn_ref[0, c * ll:(c + 1) * ll, :].astype(BF16)
        spv = None
        for gi in range(ws_ref.shape[0]):
            wsg = jnp.where(r_i >= c_i, ws_ref[gi], 0.0).astype(BF16)
            t = _dot(wsg, vnc)
            spv = t if spv is None else jnp.where(grp == gi, t, spv)
        yb_ref[0, c * ll:(c + 1) * ll, :] = zu_ref[0, c * ll:(c + 1) * ll, :] * (spv + bsp_ref[...])


def _k2_call(za, hist16, zu, vn, wp_bd, s_pool, ws, bsp, pos0, tb):
    bsz, seq, w = za.shape
    return pl.pallas_call(
        functools.partial(_k2_kernel, pos0),
        out_shape=[jax.ShapeDtypeStruct((bsz, seq, w), F32)] * 2,
        grid=(bsz, seq // tb),
        in_specs=[_tok_spec(tb, w),
                  pl.BlockSpec((1, POOL_CARRY, w), lambda b, s: (b, 0, 0)),
                  _tok_spec(tb, w), _tok_spec(tb, w),
                  _full_spec(wp_bd.shape), _full_spec(s_pool.shape), _full_spec(ws.shape),
                  _full_spec(bsp.shape)],
        out_specs=[_tok_spec(tb, w)] * 2,
        scratch_shapes=[pltpu.VMEM((POOL_CARRY, w), F32)],
        compiler_params=_cparams(("arbitrary", "arbitrary")),
        name="pool_gmlp",
    )(za, hist16, zu, vn, wp_bd, s_pool, ws, bsp)


def _lambda(lamv_ref, lam_init):
    lv = lamv_ref[...]
    a = jnp.sum(lv[0:1] * lv[1:2], axis=-1, keepdims=True)
    b = jnp.sum(lv[2:3] * lv[3:4], axis=-1, keepdims=True)
    return jnp.exp(a) - jnp.exp(b) + lam_init


def _split_maps(q):
    lane = lax.broadcasted_iota(jnp.int32, q.shape, 1)
    half = q.shape[1] // 2
    return jnp.where(lane < half, q, 0.0), jnp.where(lane >= half, q, 0.0)


def _online_update(state, s, vblk):
    m, l, acc = state
    m_new = jnp.maximum(m, jnp.max(s, axis=-1, keepdims=True))
    alpha = jnp.exp(m - m_new)
    p = jnp.exp(s - m_new)
    l_new = alpha * l + jnp.sum(p, axis=-1, keepdims=True)
    acc_new = alpha * acc + _dot(p, vblk)
    return m_new, l_new, acc_new


def _sub_norm(o0, o1, lam, lam_init, gsub):
    o = o0 - lam * o1
    return _rms_rows(o) * gsub * (1.0 - lam_init)


def _attn_prompt_kernel(lam_init, scale, q_ref, k_ref, v_ref, bias_ref, lamv_ref, gsub_ref, o_ref):
    qb = pl.program_id(2)
    tq = q_ref.shape[1]
    dv = v_ref.shape[2]
    q0, q1 = _split_maps(q_ref[0] * scale)

    def init():
        return (jnp.full((tq, 1), NEG_BIG, F32), jnp.zeros((tq, 1), F32), jnp.zeros((tq, dv), F32))

    def block(kb, bias):
        kblk = k_ref[0, pl.ds(pl.multiple_of(kb * tq, tq), tq), :].astype(BF16)
        vblk = v_ref[0, pl.ds(pl.multiple_of(kb * tq, tq), tq), :].astype(BF16)
        return _dot_nt(q0, kblk) + bias, _dot_nt(q1, kblk) + bias, vblk

    def body(kb, carry):
        st0, st1 = carry
        bias = bias_ref[0, jnp.minimum(qb - kb, 2)]
        s0, s1, vblk = block(kb, bias)
        return _online_update(st0, s0, vblk), _online_update(st1, s1, vblk)

    st0, st1 = lax.fori_loop(0, qb, body, (init(), init()))
    s0, s1, vblk = block(qb, bias_ref[0, 0])
    qi = _idiv_pow2(lax.broadcasted_iota(jnp.int32, (tq, tq), 0), ATT_CHUNK)
    ki = _idiv_pow2(lax.broadcasted_iota(jnp.int32, (tq, tq), 1), ATT_CHUNK)
    vis = ki <= qi
    st0 = _online_update(st0, jnp.where(vis, s0, NEG_BIG), vblk)
    st1 = _online_update(st1, jnp.where(vis, s1, NEG_BIG), vblk)
    lam = _lambda(lamv_ref, lam_init)
    o_ref[0] = _sub_norm(st0[2] / st0[1], st1[2] / st1[1], lam, lam_init, gsub_ref[0])


def _attn_prompt_call(qn, kn, v, bias_tiles, lamv, gsub, lam_init, heads, tq):
    bsz, seq, _ = qn.shape
    dh = qn.shape[2] // heads
    dv = v.shape[2] // heads
    scale = float((dh // 2) ** -0.5)
    return pl.pallas_call(
        functools.partial(_attn_prompt_kernel, lam_init, scale),
        out_shape=jax.ShapeDtypeStruct((bsz, seq, heads * dv), F32),
        grid=(bsz, heads, seq // tq),
        in_specs=[pl.BlockSpec((1, tq, dh), lambda b, h, i: (b, i, h)),
                  pl.BlockSpec((1, seq, dh), lambda b, h, i: (b, 0, h)),
                  pl.BlockSpec((1, seq, dv), lambda b, h, i: (b, 0, h)),
                  pl.BlockSpec((1, 3, tq, tq), lambda b, h, i: (h, 0, 0, 0)),
                  pl.BlockSpec(lamv.shape, lambda b, h, i: (0, 0)),
                  pl.BlockSpec((1, 1, dv), lambda b, h, i: (h, 0, 0))],
        out_specs=pl.BlockSpec((1, tq, dv), lambda b, h, i: (b, i, h)),
        compiler_params=_cparams(("arbitrary", "arbitrary", "arbitrary")),
        name="attn_prompt",
    )(qn, kn, v, bias_tiles, lamv, gsub)


def _attn_sample_kernel(lam_init, scale, heads, q_ref, kc_ref, vc_ref, kn_ref, vn_ref,
                        bc_ref, bn_ref, lamv_ref, gsub_ref, o_ref):
    t = q_ref.shape[1]
    dh = q_ref.shape[2] // heads
    dv = vc_ref.shape[3]
    lam = _lambda(lamv_ref, lam_init)
    pad = bn_ref.shape[2] - t
    for h in range(heads):
        q0, q1 = _split_maps(q_ref[0, :, h * dh:(h + 1) * dh] * scale)
        kc = kc_ref[0, :, h, :].astype(BF16)
        vc = vc_ref[0, :, h, :].astype(BF16)
        kn = jnp.concatenate([kn_ref[0, :, h * dh:(h + 1) * dh], jnp.zeros((pad, dh), F32)], axis=0).astype(BF16)
        vn = jnp.concatenate([vn_ref[0, :, h * dv:(h + 1) * dv], jnp.zeros((pad, dv), F32)], axis=0).astype(BF16)
        outs = []
        for qm in (q0, q1):
            sc = _dot_nt(qm, kc) + bc_ref[h]
            sn = _dot_nt(qm, kn) + bn_ref[h]
            m = jnp.maximum(jnp.max(sc, axis=-1, keepdims=True), jnp.max(sn, axis=-1, keepdims=True))
            pc = jnp.exp(sc - m)
            pn = jnp.exp(sn - m)
            l = jnp.sum(pc, axis=-1, keepdims=True) + jnp.sum(pn, axis=-1, keepdims=True)
            outs.append((_dot(pc, vc) + _dot(pn, vn)) / l)
        o_ref[0, :, h * dv:(h + 1) * dv] = _sub_norm(outs[0], outs[1], lam, lam_init, gsub_ref[h])


def _attn_sample_call(qn, cache_k, cache_v, layer, kn, vn, bias_c, bias_n, lamv, gsub, lam_init, heads):
    bsz, t, wq = qn.shape
    wv = heads * cache_v.shape[4]
    dh = wq // heads
    scale = float((dh // 2) ** -0.5)
    row = lambda shape: pl.BlockSpec((1,) + shape[1:], lambda b: (b, 0, 0))
    full = lambda shape: pl.BlockSpec(shape, lambda b: (0,) * len(shape))
    cache = lambda shape: pl.BlockSpec((None, 1) + shape[2:], lambda b: (layer, b, 0, 0, 0))
    return pl.pallas_call(
        functools.partial(_attn_sample_kernel, lam_init, scale, heads),
        out_shape=jax.ShapeDtypeStruct((bsz, t, wv), F32),
        grid=(bsz,),
        in_specs=[row(qn.shape), cache(cache_k.shape), cache(cache_v.shape), row(kn.shape), row(vn.shape),
                  full(bias_c.shape), full(bias_n.shape), full(lamv.shape), full(gsub.shape)],
        out_specs=row((bsz, t, wv)),
        compiler_params=_cparams(("arbitrary",)),
        name="attn_sample",
    )(qn, cache_k, cache_v, kn, vn, bias_c, bias_n, lamv, gsub)


def _k3_kernel(x_ref, ya_ref, yb_ref, o_ref, gt_ref, sh_ref, sc_ref, g_ref, wout_ref, wpq_ref, keys_ref,
               x1_ref, st_ref, h2t_ref):
    wa = ya_ref.shape[2]
    wb = yb_ref.shape[2]
    y = (_dot(ya_ref[0].astype(BF16), wout_ref[0:wa, :])
         + _dot(yb_ref[0].astype(BF16), wout_ref[wa:wa + wb, :])
         + _dot(o_ref[0].astype(BF16), wout_ref[wa + wb:, :]))
    x1 = x_ref[0] + gt_ref[0] * y
    x1_ref[0] = x1
    h2 = _rms_rows(x1) * g_ref[...] * (1.0 + sc_ref[0]) + sh_ref[0]
    h2t_ref[...] = h2.T.astype(BF16)
    qp = _dot(h2.astype(BF16), wpq_ref[...])
    nk = keys_ref.shape[1]
    kd = keys_ref.shape[2]
    for r in range(keys_ref.shape[0]):
        st_ref[r * nk:(r + 1) * nk, :] = _dot_nt(keys_ref[r], qp[:, r * kd:(r + 1) * kd].astype(BF16))


def _k3_call(x, ya, yb, o, gt, sh, sc, g, w_out, w_pq, keys, tm, per_token):
    bsz, seq, d = x.shape
    n = bsz * seq
    ns = seq // tm
    nrow = keys.shape[0] * keys.shape[1]
    flat = lambda b, s: (0, b * ns + s)
    return pl.pallas_call(
        _k3_kernel,
        out_shape=[jax.ShapeDtypeStruct((bsz, seq, d), F32),
                   jax.ShapeDtypeStruct((nrow, n), F32),
                   jax.ShapeDtypeStruct((d, n), BF16)],
        grid=(bsz, ns),
        in_specs=[_tok_spec(tm, d), _tok_spec(tm, ya.shape[2]), _tok_spec(tm, yb.shape[2]),
                  _tok_spec(tm, o.shape[2]),
                  _mod_spec(per_token, tm, d), _mod_spec(per_token, tm, d), _mod_spec(per_token, tm, d),
                  _full_spec(g.shape), _full_spec(w_out.shape), _full_spec(w_pq.shape),
                  _full_spec(keys.shape)],
        out_specs=[_tok_spec(tm, d), pl.BlockSpec((nrow, tm), flat), pl.BlockSpec((d, tm), flat)],
        compiler_params=_cparams(("arbitrary", "arbitrary")),
        name="out_proj_peer_scores",
    )(x, ya, yb, o, gt, sh, sc, g, w_out, w_pq, keys)


def _sort_network(n):
    def merge(lo, hi, r):
        step = r * 2
        if step < hi - lo:
            yield from merge(lo, hi, step)
            yield from merge(lo + r, hi, step)
            yield from ((i, i + r) for i in range(lo + r, hi - r, step))
        else:
            yield (lo, lo + r)

    def sort(lo, hi):
        if hi - lo >= 1:
            mid = lo + (hi - lo) // 2
            yield from sort(lo, mid)
            yield from sort(mid + 1, hi)
            yield from merge(lo, hi, 1)

    return list(sort(0, n - 1))


def _compare_exchange(vs, i, j):
    a, b = vs[i], vs[j]
    if b is None:
        return
    if a is None:
        vs[i], vs[j] = b, None
        return
    vs[i], vs[j] = jnp.maximum(a, b), jnp.minimum(a, b)


def _sort_desc(vs):
    vs = list(vs)
    for i, j in _sort_network(len(vs)):
        _compare_exchange(vs, i, j)
    return vs


def _top_across_sublanes(vs):
    n = len(vs)
    for shift in (4, 2, 1):
        mixed = []
        for k in range(n):
            a, b = vs[k], vs[n - 1 - k]
            if b is None:
                mixed.append(a)
            else:
                b = pltpu.roll(b, shift, axis=0)
                mixed.append(b if a is None else jnp.maximum(a, b))
        vs = mixed
        d = n // 2
        while d >= 1:
            for k in range(n):
                if k & d == 0:
                    _compare_exchange(vs, k, k + d)
            d //= 2
    return vs


def _all_sublanes(x, op):
    for shift in (4, 2, 1):
        x = op(x, pltpu.roll(x, shift, axis=0))
    return x


def _next_below(vals, cut, k):
    cnt = None
    below = None
    for v in vals:
        c = jnp.where(v >= cut, 1.0, 0.0)
        b = jnp.where(v < cut, v, -jnp.inf)
        cnt = c if cnt is None else cnt + c
        below = b if below is None else jnp.maximum(below, b)
    cnt = _all_sublanes(cnt, jnp.add)
    below = _all_sublanes(below, jnp.maximum)
    return jnp.where(cnt > float(k), cut, below)


PEER_TOP_ROWS = 24


def _k4_kernel(st_ref, stats_ref, top_ref):
    nlist = top_ref.shape[0]
    nk = st_ref.shape[0] // nlist
    k = PEER_TOPK
    assert nk // SUBLANES == k and k == 2 * SUBLANES

    def chunk(c, _):
        cs = pl.ds(pl.multiple_of(c * LANES, LANES), LANES)

        def lists(hh, _):
            for r in (2 * hh, 2 * hh + 1):
                base = pl.multiple_of(r * nk, nk)
                vals = [st_ref[pl.ds(base + g * SUBLANES, SUBLANES), cs] for g in range(nk // SUBLANES)]
                top = _top_across_sublanes(_sort_desc(vals))
                for i in range(k):
                    top_ref[r, i:i + 1, :] = top[i][0:1]
                top_ref[r, k:k + 1, :] = _next_below(vals, top[k - 1], k)[0:1]
            return 0

        lax.fori_loop(0, nlist // 2, lists, 0)

        def one_head(h):
            a_hi = top_ref[2 * h, SUBLANES:k]
            b_lo, b_hi = top_ref[2 * h + 1, 0:SUBLANES], top_ref[2 * h + 1, SUBLANES:k]
            a0, b0 = top_ref[2 * h, 0:1], top_ref[2 * h + 1, 0:1]
            cands = [a0 + b_lo, a0 + b_hi]
            cands += [top_ref[2 * h, p:p + 1] + b_lo for p in range(1, SUBLANES)]
            cands.append(a_hi + b0)
            tv = _top_across_sublanes(_sort_desc(cands + [None] * (k - len(cands))))
            v16 = tv[k - 1]
            v17 = _next_below(cands, v16, k)
            v17 = jnp.maximum(v17, jnp.maximum(top_ref[2 * h, k:k + 1] + b0, a0 + top_ref[2 * h + 1, k:k + 1]))
            z = jnp.ones_like(v16)
            for r in range(1, k):
                z = z + jnp.exp(tv[r] - tv[0])
            zero = jnp.zeros_like(a0)
            stats_ref[h, :, cs] = jnp.concatenate(
                [(0.5 * (v16 + v17))[0:1], a0, b0, (1.0 / z)[0:1], zero, zero, zero, zero], axis=0)

        def heads(hp, _):
            one_head(2 * hp)
            one_head(2 * hp + 1)
            return 0

        lax.fori_loop(0, nlist // 4, heads, 0)
        return 0

    lax.fori_loop(0, st_ref.shape[1] // LANES, chunk, 0)


def _k4_call(st, nlist, tt):
    nrow, n = st.shape
    nh = nlist // 2
    return pl.pallas_call(
        _k4_kernel,
        out_shape=jax.ShapeDtypeStruct((nh, SUBLANES, n), F32),
        grid=(n // tt,),
        in_specs=[pl.BlockSpec((nrow, tt), lambda i: (0, i))],
        out_specs=pl.BlockSpec((nh, SUBLANES, tt), lambda i: (0, 0, i)),
        scratch_shapes=[pltpu.VMEM((nlist, PEER_TOP_ROWS, LANES), F32)],
        compiler_params=_cparams(("arbitrary",)),
        name="peer_topk_stats",
    )(st)


PEER_ROWS = 32


PEER_PARTS = 2


def _k5_kernel(nheads, nk, ne, ht_ref, st_ref, stats_ref, u_ref, vt_ref, x_ref, gt_ref, out_ref,
               c1_ref, e1_ref, e2_ref, at0_ref, at1_ref, g0_ref, g1_ref, acc_ref):
    s = pl.program_id(0)
    sb = jnp.maximum(s - 1, 0)
    sc = jnp.maximum(s - 2, 0)
    e = sb % ne
    eb = u_ref.shape[0]
    tt = ht_ref.shape[1]
    d = vt_ref.shape[0]
    ni = eb // nk
    assert ni == SUBLANES, (eb, nk)
    grp = nk // SUBLANES
    sqrt_half = math.sqrt(0.5)

    @pl.when(s == 0)
    def _():
        for r in (at0_ref, at1_ref, g0_ref, g1_ref):
            r[...] = jnp.zeros_like(r)

    @pl.when(sc % ne == 0)
    def _():
        acc_ref[...] = jnp.zeros_like(acc_ref)

    @pl.when(e == 0)
    def _():
        for h in range(nheads):
            m1 = stats_ref[h, 1:2, :]
            thr = stats_ref[h, 0:1, :]
            for g in range(grp):
                s1 = st_ref[2 * h * nk + g * SUBLANES:2 * h * nk + (g + 1) * SUBLANES, :]
                c1_ref[h * grp + g] = thr - s1
                e1_ref[h * grp + g] = jnp.exp(s1 - m1)
            s2 = st_ref[(2 * h + 1) * nk:(2 * h + 2) * nk, :]
            e2_ref[h * nk:(h + 1) * nk, :] = jnp.exp(s2 - stats_ref[h, 2:3, :]) * (0.5 * stats_ref[h, 3:4, :])

    assert PEER_PARTS % 2 == 0 and (tt // LANES) % PEER_PARTS == 0
    msplit = PEER_PARTS // 2
    mh_rows = eb // msplit
    dh_rows = d // msplit
    nh_cols = tt // 2
    chunks = tt // LANES // PEER_PARTS

    def part(at_w, at_r, g_w, g_r, k, _):
        mh = k // 2
        nc = pl.ds(pl.multiple_of((k % 2) * nh_cols, nh_cols), nh_cols)
        er = pl.ds(pl.multiple_of(mh * mh_rows, mh_rows), mh_rows)
        at_w[er, nc] = _dot(u_ref[er, :], ht_ref[:, nc])
        for c in range(chunks):
            cs = pl.ds(pl.multiple_of((k * chunks + c) * LANES, LANES), LANES)
            for j0 in range(0, nk, PEER_ROWS):
                w = [jnp.zeros((PEER_ROWS, LANES), F32) for _ in range(ni)]
                for h in range(nheads):
                    s2 = st_ref[(2 * h + 1) * nk + j0:(2 * h + 1) * nk + j0 + PEER_ROWS, cs]
                    e2 = e2_ref[h * nk + j0:h * nk + j0 + PEER_ROWS, cs]
                    for ii in range(ni):
                        sel = s2 >= c1_ref[h * grp + e, ii:ii + 1, cs]
                        w[ii] = w[ii] + e1_ref[h * grp + e, ii:ii + 1, cs] * jnp.where(sel, e2, 0.0)
                for ii in range(ni):
                    rows = slice(ii * nk + j0, ii * nk + j0 + PEER_ROWS)
                    a = at_r[rows, cs]
                    g_w[rows, cs] = ((w[ii] * a) * (1.0 + lax.erf(a * sqrt_half))).astype(BF16)
        dr = pl.ds(pl.multiple_of(mh * dh_rows, dh_rows), dh_rows)
        acc_ref[dr, nc] += _dot(vt_ref[dr, :], g_r[:, nc])
        return 0

    @pl.when(s % 2 == 0)
    def _():
        lax.fori_loop(0, PEER_PARTS, functools.partial(part, at0_ref, at1_ref, g1_ref, g0_ref), 0)

    @pl.when(s % 2 == 1)
    def _():
        lax.fori_loop(0, PEER_PARTS, functools.partial(part, at1_ref, at0_ref, g0_ref, g1_ref), 0)

    @pl.when(jnp.logical_and(s >= 2, sc % ne == ne - 1))
    def _():
        out_ref[...] = x_ref[...] + gt_ref[0] * acc_ref[...].T


def _k5_call(h2t, st, stats, u_all, layer, vt, x, gt, per_token, nheads, nk, tt, eb):
    d, n = h2t.shape
    assert vt.shape[1:] == (d, eb), vt.shape
    seq = x.shape[1]
    assert seq % tt == 0
    tiles_per_row = seq // tt
    ne = u_all.shape[1] // eb
    nt = n // tt
    tile_a = lambda s: jnp.minimum(s // ne, nt - 1)
    tile_b = lambda s: jnp.minimum(jnp.maximum(s - 1, 0) // ne, nt - 1)
    tile_c = lambda s: jnp.minimum(jnp.maximum(s - 2, 0) // ne, nt - 1)
    return pl.pallas_call(
        functools.partial(_k5_kernel, nheads, nk, ne),
        out_shape=jax.ShapeDtypeStruct((n, d), F32),
        grid=(nt * ne + 2,),
        in_specs=[pl.BlockSpec((d, tt), lambda s: (0, tile_a(s))),
                  pl.BlockSpec((st.shape[0], tt), lambda s: (0, tile_b(s))),
                  pl.BlockSpec((nheads, SUBLANES, tt), lambda s: (0, 0, tile_b(s))),
                  pl.BlockSpec((None, eb, d), lambda s: (layer, s % ne, 0)),
                  pl.BlockSpec((None, d, eb), lambda s: (jnp.maximum(s - 2, 0) % ne, 0, 0)),
                  pl.BlockSpec((tt, d), lambda s: (tile_c(s), 0)),
                  (pl.BlockSpec((1, tt, d), lambda s: (0, tile_c(s), 0)) if per_token else
                   pl.BlockSpec((1, 1, d), lambda s: (tile_c(s) // tiles_per_row, 0, 0)))],
        out_specs=pl.BlockSpec((tt, d), lambda s: (tile_c(s), 0)),
        scratch_shapes=[pltpu.VMEM((nheads * nk // SUBLANES, SUBLANES, tt), F32),
                        pltpu.VMEM((nheads * nk // SUBLANES, SUBLANES, tt), F32),
                        pltpu.VMEM((nheads * nk, tt), F32),
                        pltpu.VMEM((eb, tt), F32), pltpu.VMEM((eb, tt), F32),
                        pltpu.VMEM((eb, tt), BF16), pltpu.VMEM((eb, tt), BF16),
                        pltpu.VMEM((d, tt), F32)],
        compiler_params=_cparams(("arbitrary",)),
        name="peer_dense",
    )(h2t, st, stats, u_all, vt, x.reshape(n, d), gt)


def _t5_bucket(rel):
    nb = REL_BUCKETS // 2
    max_exact = nb // 2
    side = jnp.where(rel > 0, nb, 0)
    n = jnp.abs(rel)
    nf = jnp.maximum(n, 1).astype(jnp.float32)
    large = max_exact + (jnp.log(nf / max_exact) / math.log(REL_MAX_DIST / max_exact)
                         * (nb - max_exact)).astype(jnp.int32)
    large = jnp.minimum(large, nb - 1)
    return side + jnp.where(n < max_exact, n, large)


def _bias_table(rel_bias, q_pos, k_pos):
    bucket = _t5_bucket(k_pos[None, :] - q_pos[:, None])
    rb = rel_bias.astype(F32)
    out = jnp.zeros((rb.shape[1],) + bucket.shape, F32)
    for b in range(rb.shape[0]):
        out = jnp.where(bucket[None] == b, rb[b][:, None, None], out)
    return out


def _block_diag(w):
    g, a, b = w.shape
    eye = jnp.eye(g, dtype=w.dtype)
    return (eye[:, None, :, None] * w[:, :, None, :]).reshape(g * a, g * b)


def _trunk(x, c_mod, per_token, sample, layer_w, rel_bias, tm, tq, tt, eb):
    bsz, seq, d = x.shape
    depth = len(layer_w)
    k_out, v_out, pool_out, gv_out = [], [], [], []
    tok = (lambda t: t.reshape(1, bsz * seq, t.shape[-1])) if per_token else (lambda t: t)
    seqv = lambda t: t.reshape(bsz, seq, t.shape[-1])
    xt = tok(x)
    if sample is None:
        pos = jnp.arange(tq)
        tiles = jnp.stack([_bias_table(rel_bias, pos, pos - dlt * tq) for dlt in range(3)], axis=1)
    else:
        past = sample[1].shape[2]
        q_pos = past + jnp.arange(seq)
        bias_c = _bias_table(rel_bias, q_pos, jnp.arange(past))
        bias_n = jnp.pad(_bias_table(rel_bias, q_pos, q_pos), ((0, 0), (0, 0), (0, LANES - seq)),
                         constant_values=NEG_BIG)
    for l in range(depth):
        w = layer_w[l]
        sh1, sc1, gt1, sh2, sc2, gt2 = c_mod[l]
        heads = w["heads"]
        widths = w["widths"]
        outs = _k1_call(xt, sh1, sc1, w["g1"], w["w_in"], w["p256"], w["g_gv"],
                        w["gq"], w["gk"], widths, tm, per_token)
        za, zu, vn, qn, kn, v = [seqv(t) for t in outs]
        if sample is None:
            hist = jnp.zeros((bsz, POOL_CARRY, za.shape[2]), F32)
            ya, yb = _k2_call(za, hist, zu, vn, w["wp_bd"], w["s_pool"], w["w_s"], w["bsp"], 0, 2 * GMLP_CHUNK)
            pool_out.append(za[:, seq - (POOL_CARRY - 1):])
        else:
            state_pool, cache_k, cache_v = sample
            hist = jnp.pad(state_pool[l], ((0, 0), (1, 0), (0, 0)))
            padr = ((0, 0), (0, GMLP_CHUNK - seq), (0, 0))
            ya, yb = _k2_call(jnp.pad(za, padr), hist, jnp.pad(zu, padr), jnp.pad(vn, padr), w["wp_bd"],
                              w["s_pool"], w["w_s"], w["bsp"], cache_k.shape[2], GMLP_CHUNK)
            ya, yb = ya[:, :seq], yb[:, :seq]
            pool_out.append(jnp.concatenate([state_pool[l], za], axis=1)[:, -(POOL_CARRY - 1):])
            gv_out.append(vn)
        lam_init = 0.8 - 0.6 * math.exp(-0.3 * l)
        if sample is None:
            o = _attn_prompt_call(qn, kn, v, tiles, w["lamv"], w["g_sub"], lam_init, heads, tq)
        else:
            o = _attn_sample_call(qn, cache_k, cache_v, l, kn, v, bias_c, bias_n, w["lamv"], w["g_sub"],
                                  lam_init, heads)
        k_out.append(kn.reshape(bsz, seq, heads, -1))
        v_out.append(v.reshape(bsz, seq, heads, -1))
        xt, st, h2t = _k3_call(xt, tok(ya), tok(yb), tok(o), gt1, sh2, sc2, w["g2"], w["w_out"], w["w_pq"],
                               w["keys"], tm, per_token)
        nlist = w["keys"].shape[0]
        nk = w["keys"].shape[1]
        stats = _k4_call(st, nlist, tt)
        xt = _k5_call(h2t, st, stats, w["u_all"], l, w["vt"], xt, gt2, per_token, nlist // 2, nk, tt,
                      eb).reshape(xt.shape)
    y = xt.reshape(bsz, seq, d)
    return y, jnp.stack(k_out), jnp.stack(v_out), jnp.stack(pool_out), (jnp.stack(gv_out) if gv_out else None)


def kernel(x_prompt, x_sample, c_prompt, c_sample, cache_k, cache_v, state_pool, rel_bias, w_ada, b_ada,
           g_norm, w_in, w_out, w_pool, s_pool, g_gv, w_s, b_s, g_qk, lam_vecs, g_sub, w_pq, sub_keys,
           u_tab, v_tab):
    depth = w_in.shape[0]
    d = x_prompt.shape[2]
    bp, sp = x_prompt.shape[:2]
    bs, ss = x_sample.shape[:2]
    heads = cache_k.shape[3]
    dqk = cache_k.shape[4]
    dv = cache_v.shape[4]
    wa = w_pool.shape[1] * w_pool.shape[2]
    wg = g_gv.shape[1]
    widths = (wa, wg, heads * dqk, heads * dv)
    grp = dqk // 2

    p256 = _block_diag(jnp.full((256 // grp, grp, grp), 1.0 / grp, F32)).astype(BF16)
    c_all = jnp.concatenate([c_prompt, c_sample], axis=0)
    layer_w, mods_p, mods_s = [], [], []
    for l in range(depth):
        mod = _ada_call(c_all, w_ada[l], b_ada[l])
        six = jnp.split(mod, 6, axis=-1)
        mods_p.append([m[:bp, None, :] for m in six])
        mods_s.append([jnp.broadcast_to(m[bp:, None, :], (bs, ss, d)).reshape(1, bs * ss, d) for m in six])
        nh, two, nk, kd = sub_keys.shape[1:]
        layer_w.append(dict(
            heads=heads, widths=widths,
            g1=g_norm[l, 0].reshape(1, d), g2=g_norm[l, 1].reshape(1, d),
            w_in=w_in[l].astype(BF16), w_out=w_out[l].astype(BF16), w_pq=w_pq[l].astype(BF16),
            p256=p256, g_gv=g_gv[l].reshape(1, wg),
            gq=jnp.tile(g_qk[l, 0], heads * dqk // grp).reshape(1, heads * dqk),
            gk=jnp.tile(g_qk[l, 1], heads * dqk // grp).reshape(1, heads * dqk),
            wp_bd=_block_diag(w_pool[l]).astype(BF16), s_pool=s_pool[l].reshape(1, wa),
            w_s=w_s[l], bsp=jnp.repeat(b_s[l].T, wg // b_s.shape[1], axis=1),
            lamv=lam_vecs[l], g_sub=g_sub[l].reshape(heads, 1, dv),
            keys=sub_keys[l].reshape(nh * two, nk, kd).astype(BF16),
            u_all=u_tab, vt=jnp.transpose(v_tab[l].reshape(-1, SUBLANES * nk, d), (0, 2, 1))))

    tm = min(512, sp)
    y_p, k_p, v_p, pool_p, _ = _trunk(x_prompt, mods_p, False, None, layer_w, rel_bias,
                                      tm=tm, tq=min(512, sp), tt=min(512, bp * sp), eb=SUBLANES * sub_keys.shape[3])
    y_s, k_s, v_s, pool_s, gv_s = _trunk(x_sample, mods_s, True, (state_pool, cache_k, cache_v), layer_w,
                                         rel_bias, tm=min(256, bs * ss), tq=None, tt=min(512, bs * ss), eb=SUBLANES * sub_keys.shape[3])
    return (y_p, y_s, k_p, v_p, pool_p, k_s, v_s, pool_s, gv_s)
```
